```python
import math
import jax, jax.numpy as jnp
from jax import lax
import numpy as np

D_MODEL = 1024
BATCH = 16
SEQ = 4096
DEPTH = 2

N_MIXERS = 2
N_RET = (DEPTH + 1) // 2
N_S5 = DEPTH // 2
RET_HEADS = 4
RET_DK = D_MODEL // RET_HEADS
RET_DV = 2 * RET_DK
RET_QK = RET_HEADS * RET_DK
RET_V = RET_HEADS * RET_DV
RET_PROJ = 2 * RET_QK + 2 * RET_V
RET_CHUNK = 128
ROPE_BASE = 10000.0
S5_GROUP = 16
S5_GROUPS = D_MODEL // S5_GROUP
S5_STATE = 64
DT_MIN = 1e-3
DT_MAX = 1e-1
D_FF = 2816
CONV_W = 3
EPS = 1e-6

kernel_name = "hybrid_retention_s5_convffn_adaln"


def rms_norm(x):
    xf = x.astype(jnp.float32)
    return (xf * lax.rsqrt(jnp.mean(xf * xf, axis=-1, keepdims=True) + EPS)).astype(x.dtype)


def rotary(t, pos):
    half = t.shape[-1] // 2
    inv_freq = jnp.power(ROPE_BASE, -jnp.arange(half, dtype=jnp.float32) / half)
    ang = pos.astype(jnp.float32)[..., None] * inv_freq
    cos = jnp.cos(ang)[:, :, None, :]
    sin = jnp.sin(ang)[:, :, None, :]
    t1 = t[..., :half].astype(jnp.float32)
    t2 = t[..., half:].astype(jnp.float32)
    return jnp.concatenate([t1 * cos - t2 * sin, t1 * sin + t2 * cos], axis=-1).astype(t.dtype)


def retention_chunkwise(q, k, v):
    bsz, L = q.shape[0], q.shape[1]
    nc = L // RET_CHUNK
    log_gamma = jnp.log1p(-jnp.exp2(-5.0 - jnp.arange(RET_HEADS, dtype=jnp.float32)))
    idx = jnp.arange(RET_CHUNK, dtype=jnp.float32)
    diff = idx[:, None] - idx[None, :]
    intra = jnp.where(diff[None] >= 0,
                      jnp.exp(log_gamma[:, None, None] * jnp.maximum(diff, 0.0)[None]), 0.0)
    cross = jnp.exp(log_gamma[:, None] * (idx + 1.0))
    to_state = jnp.exp(log_gamma[:, None] * (RET_CHUNK - 1.0 - idx))
    chunk_decay = jnp.exp(log_gamma * RET_CHUNK)

    def to_chunks(t):
        return t.reshape(bsz, nc, RET_CHUNK, RET_HEADS, t.shape[-1]).transpose(1, 0, 3, 2, 4)

    def step(state, qkv):
        qc, kc, vc = qkv
        s = jnp.einsum('bhnd,bhmd->bhnm', qc, kc) * intra[None]
        o = (jnp.einsum('bhnm,bhmv->bhnv', s, vc)
             + jnp.einsum('bhnd,bhdv->bhnv', qc, state) * cross[None, :, :, None])
        state = (chunk_decay[None, :, None, None] * state
                 + jnp.einsum('bhmd,bhmv->bhdv', kc * to_state[None, :, :, None], vc))
        return state, o

    state0 = jnp.zeros((bsz, RET_HEADS, RET_DK, RET_DV), jnp.float32)
    _, o = lax.scan(step, state0, (to_chunks(q), to_chunks(k), to_chunks(v)))
    return o.transpose(1, 0, 3, 2, 4).reshape(bsz, L, RET_HEADS, RET_DV)


def retention_mixer(h, pos, w_in, w_out):
    bsz, L, _ = h.shape
    proj = h @ w_in
    q, k, v, g = jnp.split(proj, [RET_QK, 2 * RET_QK, 2 * RET_QK + RET_V], axis=-1)
    q = rotary(q.reshape(bsz, L, RET_HEADS, RET_DK), pos)
    k = rotary(k.reshape(bsz, L, RET_HEADS, RET_DK), pos) * (RET_DK ** -0.5)
    v = v.reshape(bsz, L, RET_HEADS, RET_DV)
    o = retention_chunkwise(q.astype(jnp.float32), k.astype(jnp.float32), v.astype(jnp.float32))
    mu = jnp.mean(o, axis=-1, keepdims=True)
    oc = o - mu
    o = oc * lax.rsqrt(jnp.mean(oc * oc, axis=-1, keepdims=True) + EPS)
    o = o.reshape(bsz, L, RET_V).astype(h.dtype)
    return (jax.nn.silu(g) * o) @ w_out


def s5_mixer(h, w_in, lam_re, lam_im, log_dt, b_re, b_im, c_re, c_im, d_skip, w_glu):
    bsz, L, _ = h.shape
    u = (h @ w_in).astype(jnp.float32)
    ug = u.reshape(bsz, L, S5_GROUPS, S5_GROUP)
    dt = jnp.exp(log_dt.astype(jnp.float32))[:, None]
    lr = lam_re.astype(jnp.float32)
    li = lam_im.astype(jnp.float32)
    mag = jnp.exp(lr * dt)
    ar = mag * jnp.cos(li * dt)
    ai = mag * jnp.sin(li * dt)
    nr = ar - 1.0
    den = lr * lr + li * li
    fr = (nr * lr + ai * li) / den
    fi = (ai * lr - nr * li) / den
    br = b_re.astype(jnp.float32)
    bi = b_im.astype(jnp.float32)
    bbr = fr[..., None] * br - fi[..., None] * bi
    bbi = fr[..., None] * bi + fi[..., None] * br
    bu_r = jnp.einsum('blgi,gpi->lbgp', ug, bbr)
    bu_i = jnp.einsum('blgi,gpi->lbgp', ug, bbi)
    a_r = jnp.broadcast_to(ar[None, None], (L, 1, S5_GROUPS, S5_STATE))
    a_i = jnp.broadcast_to(ai[None, None], (L, 1, S5_GROUPS, S5_STATE))

    def combine(e1, e2):
        a1r, a1i, b1r, b1i = e1
        a2r, a2i, b2r, b2i = e2
        return (a2r * a1r - a2i * a1i,
                a2r * a1i + a2i * a1r,
                a2r * b1r - a2i * b1i + b2r,
                a2r * b1i + a2i * b1r + b2i)

    _, _, s_r, s_i = lax.associative_scan(combine, (a_r, a_i, bu_r, bu_i), axis=0)
    y = (jnp.einsum('lbgp,gip->blgi', s_r, c_re.astype(jnp.float32))
         - jnp.einsum('lbgp,gip->blgi', s_i, c_im.astype(jnp.float32)))
    y = y.reshape(bsz, L, D_MODEL) + d_skip.astype(jnp.float32) * u
    y = jax.nn.gelu(y).astype(h.dtype)
    ya, yb = jnp.split(y @ w_glu, 2, axis=-1)
    return ya * jax.nn.sigmoid(yb)


def conv_ffn(h, w_up, conv_w, conv_b, w_down):
    val, gate = jnp.split(h @ w_up, 2, axis=-1)
    gate = lax.conv_general_dilated(gate, conv_w.astype(gate.dtype), window_strides=(1,),
                                    padding=[(CONV_W - 1, 0)],
                                    dimension_numbers=('NWC', 'WIO', 'NWC'),
                                    feature_group_count=D_FF) + conv_b
    return (jax.nn.silu(gate) * val) @ w_down


def setup_inputs(seed: int = 0) -> dict:
    key = jax.random.key(seed)
    ks = jax.random.split(key, 24)
    f32 = jnp.float32
    nrm = lambda k, shape, s: jax.random.normal(k, shape, f32) * s
    x = nrm(ks[0], (BATCH, SEQ, D_MODEL), 1.0)
    c = nrm(ks[1], (BATCH, D_MODEL), 1.0)
    offset = jax.random.randint(ks[2], (BATCH, 1), 0, 1024, dtype=jnp.int32)
    pos = offset + jnp.arange(SEQ, dtype=jnp.int32)[None, :]
    ada_w = nrm(ks[3], (DEPTH, D_MODEL, 6 * D_MODEL), D_MODEL ** -0.5)
    ada_b = nrm(ks[4], (DEPTH, 6 * D_MODEL), 0.02)
    ret_w_in = nrm(ks[5], (N_RET, D_MODEL, RET_PROJ), D_MODEL ** -0.5)
    ret_w_out = nrm(ks[6], (N_RET, RET_V, D_MODEL), RET_V ** -0.5)
    s5_w_in = nrm(ks[7], (N_S5, D_MODEL, D_MODEL), D_MODEL ** -0.5)
    s5_lam_re = -0.5 * jnp.exp(nrm(ks[8], (N_S5, S5_GROUPS, S5_STATE), 0.02))
    s5_lam_im = (math.pi * jnp.arange(S5_STATE, dtype=f32))[None, None, :] + nrm(ks[9], (N_S5, S5_GROUPS, S5_STATE), 0.01)
    s5_log_dt = jax.random.uniform(ks[10], (N_S5, S5_GROUPS), f32, math.log(DT_MIN), math.log(DT_MAX))
    s5_b_re = nrm(ks[11], (N_S5, S5_GROUPS, S5_STATE, S5_GROUP), (2 * S5_GROUP) ** -0.5)
    s5_b_im = nrm(ks[12], (N_S5, S5_GROUPS, S5_STATE, S5_GROUP), (2 * S5_GROUP) ** -0.5)
    s5_c_re = nrm(ks[13], (N_S5, S5_GROUPS, S5_GROUP, S5_STATE), S5_STATE ** -0.5)
    s5_c_im = nrm(ks[14], (N_S5, S5_GROUPS, S5_GROUP, S5_STATE), S5_STATE ** -0.5)
    s5_d = nrm(ks[15], (N_S5, D_MODEL), 1.0)
    s5_w_glu = nrm(ks[16], (N_S5, D_MODEL, 2 * D_MODEL), D_MODEL ** -0.5)
    ffn_w_up = nrm(ks[17], (DEPTH, D_MODEL, 2 * D_FF), D_MODEL ** -0.5)
    ffn_conv_w = nrm(ks[18], (DEPTH, CONV_W, 1, D_FF), CONV_W ** -0.5)
    ffn_conv_b = nrm(ks[19], (DEPTH, D_FF), 0.02)
    ffn_w_down = nrm(ks[20], (DEPTH, D_FF, D_MODEL), D_FF ** -0.5)
    final_norm_g = 1.0 + nrm(ks[21], (D_MODEL,), 0.02)
    return {"x": x, "c": c, "pos": pos, "ada_w": ada_w, "ada_b": ada_b,
            "ret_w_in": ret_w_in, "ret_w_out": ret_w_out,
            "s5_w_in": s5_w_in, "s5_lam_re": s5_lam_re, "s5_lam_im": s5_lam_im,
            "s5_log_dt": s5_log_dt, "s5_b_re": s5_b_re, "s5_b_im": s5_b_im,
            "s5_c_re": s5_c_re, "s5_c_im": s5_c_im, "s5_d": s5_d, "s5_w_glu": s5_w_glu,
            "ffn_w_up": ffn_w_up, "ffn_conv_w": ffn_conv_w, "ffn_conv_b": ffn_conv_b,
            "ffn_w_down": ffn_w_down, "final_norm_g": final_norm_g}


def reference(x, c, pos, ada_w, ada_b, ret_w_in, ret_w_out, s5_w_in, s5_lam_re, s5_lam_im,
              s5_log_dt, s5_b_re, s5_b_im, s5_c_re, s5_c_im, s5_d, s5_w_glu,
              ffn_w_up, ffn_conv_w, ffn_conv_b, ffn_w_down, final_norm_g):
    cond = jax.nn.silu(c)
    for i in range(DEPTH):
        mod = cond @ ada_w[i] + ada_b[i]
        sh1, sc1, g1, sh2, sc2, g2 = jnp.split(mod[:, None, :], 6, axis=-1)
        h = rms_norm(x) * (1.0 + sc1) + sh1
        j = i // N_MIXERS
        if i % N_MIXERS == 0:
            y = retention_mixer(h, pos, ret_w_in[j], ret_w_out[j])
        else:
            y = s5_mixer(h, s5_w_in[j], s5_lam_re[j], s5_lam_im[j], s5_log_dt[j],
                         s5_b_re[j], s5_b_im[j], s5_c_re[j], s5_c_im[j], s5_d[j], s5_w_glu[j])
        x = x + g1 * y
        h = rms_norm(x) * (1.0 + sc2) + sh2
        x = x + g2 * conv_ffn(h, ffn_w_up[i], ffn_conv_w[i], ffn_conv_b[i], ffn_w_down[i])
    return rms_norm(x) * final_norm_g
```

```python
import functools
import math

import jax
import jax.numpy as jnp
from jax import lax
from jax.experimental import pallas as pl
from jax.experimental.pallas import tpu as pltpu

F32 = jnp.float32
BF16 = jnp.bfloat16

D_MODEL = 1024
BATCH = 16
SEQ = 4096
DEPTH = 2
RET_HEADS = 4
RET_DK = 256
RET_DV = 512
RET_QK = RET_HEADS * RET_DK
RET_V = RET_HEADS * RET_DV
RET_PROJ = 2 * RET_QK + 2 * RET_V
RET_CHUNK = 128
ROPE_BASE = 10000.0
S5_GROUP = 16
S5_GROUPS = 64
S5_STATE = 64
D_FF = 2816
CONV_W = 3
EPS = 1e-6

LANES = 128
SUBLANES = 8
VMEM_LIMIT_BYTES = 56 * 1024 * 1024

PROJ_ROWS = 512
RET_ROWS = 512
TM_STEPS = 32
TM_ROWS = TM_STEPS * BATCH
FF_COLS = 1408
S5_BLOCKS = D_MODEL // LANES
S5_BLOCK_STATES = (LANES // S5_GROUP) * S5_STATE
MOD_COLS = 1536


def _resident(shape):
    zeros = (0,) * len(shape)
    return pl.BlockSpec(shape, lambda *_: zeros, pipeline_mode=pl.Buffered(1))


def _params(n_axes):
    return pltpu.CompilerParams(
        dimension_semantics=("arbitrary",) * n_axes,
        vmem_limit_bytes=VMEM_LIMIT_BYTES,
    )


def _sigmoid(x):
    return jax.nn.sigmoid(x)


def _norm_mod(x, scale, shift):
    ms = jnp.mean(x * x, axis=-1, keepdims=True)
    return x * lax.rsqrt(ms + EPS) * (1.0 + scale) + shift


def _mod_kernel(c_ref, w_ref, b_ref, o_ref):
    c = c_ref[...]
    cond = c * _sigmoid(c)
    o_ref[0] = jnp.dot(cond, w_ref[0], preferred_element_type=F32,
                       precision=lax.Precision.HIGHEST) + b_ref[0]


def _modulation(c, ada_w, ada_b):
    n_cols = 6 * D_MODEL
    return pl.pallas_call(
        _mod_kernel,
        out_shape=jax.ShapeDtypeStruct((DEPTH, BATCH, n_cols), F32),
        grid=(DEPTH, n_cols // MOD_COLS),
        in_specs=[
            pl.BlockSpec((BATCH, D_MODEL), lambda i, j: (0, 0)),
            pl.BlockSpec((1, D_MODEL, MOD_COLS), lambda i, j: (i, 0, j)),
            pl.BlockSpec((1, 1, MOD_COLS), lambda i, j: (i, 0, j)),
        ],
        out_specs=pl.BlockSpec((1, BATCH, MOD_COLS), lambda i, j: (i, 0, j)),
        compiler_params=_params(2),
        name="adaln_modulation",
    )(c, ada_w, ada_b.reshape(DEPTH, 1, n_cols))


def _ret_proj_kernel(x_ref, pos_ref, mod_ref, invf_ref, w_ref,
                     q_ref, k_ref, v_ref, sg_ref, h_ref):
    shift = mod_ref[0, :, 0:D_MODEL]
    scale = mod_ref[0, :, D_MODEL:2 * D_MODEL]
    h_ref[...] = _norm_mod(x_ref[0], scale, shift).astype(BF16)

    ang = pos_ref[0].astype(F32) * invf_ref[...]
    cos = jnp.cos(ang)
    sin = jnp.sin(ang)
    k_scale = RET_DK ** -0.5
    cos_k = cos * k_scale
    sin_k = sin * k_scale
    half = RET_DK // 2

    for hd in range(RET_HEADS):
        for dst, base, cs, sn in ((q_ref, 0, cos, sin), (k_ref, RET_QK, cos_k, sin_k)):
            c0 = base + hd * RET_DK
            y = jnp.dot(h_ref[...], w_ref[:, c0:c0 + RET_DK], preferred_element_type=F32)
            t1 = y[:, :half]
            t2 = y[:, half:]
            o0 = hd * RET_DK
            dst[0, :, o0:o0 + half] = (t1 * cs - t2 * sn).astype(BF16)
            dst[0, :, o0 + half:o0 + RET_DK] = (t1 * sn + t2 * cs).astype(BF16)

    for hd in range(RET_HEADS):
        c0 = 2 * RET_QK + hd * RET_DV
        v_ref[0, :, hd * RET_DV:(hd + 1) * RET_DV] = jnp.dot(
            h_ref[...], w_ref[:, c0:c0 + RET_DV], preferred_element_type=F32).astype(BF16)
    for hd in range(RET_HEADS):
        c0 = 2 * RET_QK + RET_V + hd * RET_DV
        g = jnp.dot(h_ref[...], w_ref[:, c0:c0 + RET_DV], preferred_element_type=F32)
        sg_ref[0, :, hd * RET_DV:(hd + 1) * RET_DV] = (g * _sigmoid(g)).astype(BF16)


def _ret_projection(x, pos, mod0, inv_freq, w_in):
    rows = PROJ_ROWS
    row_map = lambda b, t: (b, t, 0)
    return pl.pallas_call(
        _ret_proj_kernel,
        out_shape=(
            jax.ShapeDtypeStruct((BATCH, SEQ, RET_QK), BF16),
            jax.ShapeDtypeStruct((BATCH, SEQ, RET_QK), BF16),
            jax.ShapeDtypeStruct((BATCH, SEQ, RET_V), BF16),
            jax.ShapeDtypeStruct((BATCH, SEQ, RET_V), BF16),
        ),
        grid=(BATCH, SEQ // rows),
        in_specs=[
            pl.BlockSpec((1, rows, D_MODEL), row_map),
            pl.BlockSpec((1, rows, 1), row_map),
            pl.BlockSpec((1, 1, 6 * D_MODEL), lambda b, t: (b, 0, 0)),
            _resident((1, RET_DK // 2)),
            _resident((D_MODEL, RET_PROJ)),
        ],
        out_specs=(
            pl.BlockSpec((1, rows, RET_QK), row_map),
            pl.BlockSpec((1, rows, RET_QK), row_map),
            pl.BlockSpec((1, rows, RET_V), row_map),
            pl.BlockSpec((1, rows, RET_V), row_map),
        ),
        scratch_shapes=[pltpu.VMEM((rows, D_MODEL), BF16)],
        compiler_params=_params(2),
        name="retention_projection",
    )(x, pos.reshape(BATCH, SEQ, 1), mod0.reshape(BATCH, 1, 6 * D_MODEL), inv_freq, w_in)


def _ret_core_kernel(q_ref, k_ref, v_ref, sg_ref, x_ref, mod_ref, intra_ref, cross_ref,
                     tost_ref, decay_ref, wout_ref, o_ref, state_ref, go_ref):
    @pl.when(pl.program_id(1) == 0)
    def _():
        state_ref[...] = jnp.zeros_like(state_ref)

    contract_last = (((1,), (1,)), ((), ()))
    contract_first = (((0,), (0,)), ((), ()))
    for hd in range(RET_HEADS):
        qk_cols = slice(hd * RET_DK, (hd + 1) * RET_DK)
        v_cols = slice(hd * RET_DV, (hd + 1) * RET_DV)
        for c in range(RET_ROWS // RET_CHUNK):
            rows = slice(c * RET_CHUNK, (c + 1) * RET_CHUNK)
            qc = q_ref[0, rows, qk_cols]
            kc = k_ref[0, rows, qk_cols]
            vc = v_ref[0, rows, v_cols]
            state = state_ref[hd]
            s = lax.dot_general(qc, kc, contract_last, preferred_element_type=F32) * intra_ref[hd]
            o = jnp.dot(s.astype(BF16), vc, preferred_element_type=F32)
            o = o + jnp.dot(qc, state.astype(BF16), preferred_element_type=F32) * cross_ref[hd]
            kd = (kc.astype(F32) * tost_ref[hd]).astype(BF16)
            state_ref[hd] = decay_ref[hd] * state + lax.dot_general(
                kd, vc, contract_first, preferred_element_type=F32)
            mu = jnp.mean(o, axis=-1, keepdims=True)
            oc = o - mu
            on = oc * lax.rsqrt(jnp.mean(oc * oc, axis=-1, keepdims=True) + EPS)
            go_ref[rows, v_cols] = (sg_ref[0, rows, v_cols].astype(F32) * on).astype(BF16)

    y = jnp.dot(go_ref[...], wout_ref[...], preferred_element_type=F32)
    gate = mod_ref[0, :, 2 * D_MODEL:3 * D_MODEL]
    o_ref[0] = x_ref[0] + gate * y


def _retention_core(q, k, v, sg, x, mod0, w_out):
    log_gamma = jnp.log1p(-jnp.exp2(-5.0 - jnp.arange(RET_HEADS, dtype=F32)))
    idx = jnp.arange(RET_CHUNK, dtype=F32)
    diff = idx[:, None] - idx[None, :]
    intra = jnp.where(diff[None] >= 0,
                      jnp.exp(log_gamma[:, None, None] * jnp.maximum(diff, 0.0)[None]), 0.0)
    cross = jnp.exp(log_gamma[:, None] * (idx + 1.0))[:, :, None]
    to_state = jnp.exp(log_gamma[:, None] * (RET_CHUNK - 1.0 - idx))[:, :, None]
    chunk_decay = jnp.exp(log_gamma * RET_CHUNK)

    rows = RET_ROWS
    row_map = lambda b, t: (b, t, 0)
    return pl.pallas_call(
        _ret_core_kernel,
        out_shape=jax.ShapeDtypeStruct((BATCH, SEQ, D_MODEL), F32),
        grid=(BATCH, SEQ // rows),
        in_specs=[
            pl.BlockSpec((1, rows, RET_QK), row_map),
            pl.BlockSpec((1, rows, RET_QK), row_map),
            pl.BlockSpec((1, rows, RET_V), row_map),
            pl.BlockSpec((1, rows, RET_V), row_map),
            pl.BlockSpec((1, rows, D_MODEL), row_map),
            pl.BlockSpec((1, 1, 6 * D_MODEL), lambda b, t: (b, 0, 0)),
            _resident((RET_HEADS, RET_CHUNK, RET_CHUNK)),
            _resident((RET_HEADS, RET_CHUNK, 1)),
            _resident((RET_HEADS, RET_CHUNK, 1)),
            pl.BlockSpec(memory_space=pltpu.SMEM),
            _resident((RET_V, D_MODEL)),
        ],
        out_specs=pl.BlockSpec((1, rows, D_MODEL), row_map),
        scratch_shapes=[
            pltpu.VMEM((RET_HEADS, RET_DK, RET_DV), F32),
            pltpu.VMEM((rows, RET_V), BF16),
        ],
        compiler_params=_params(2),
        name="retention_core",
    )(q, k, v, sg, x, mod0.reshape(BATCH, 1, 6 * D_MODEL), intra, cross, to_state,
      chunk_decay, w_out)


def _ffn_kernel(x_ref, mod_ref, wup_ref, cw_ref, cb_ref, wdn_ref, fg_ref, o_ref,
                h_ref, gate_ref, *, final_norm):
    halo = (CONV_W - 1) * BATCH

    @pl.when(pl.program_id(0) == 0)
    def _():
        gate_ref[0:halo, :] = jnp.zeros((halo, D_FF), F32)

    x = x_ref[...]
    shift = mod_ref[:, 3 * D_MODEL:4 * D_MODEL][None]
    scale = mod_ref[:, 4 * D_MODEL:5 * D_MODEL][None]
    res_gate = mod_ref[:, 5 * D_MODEL:6 * D_MODEL][None]
    h_ref[...] = _norm_mod(x, scale, shift).reshape(TM_ROWS, D_MODEL).astype(BF16)

    acc = jnp.zeros((TM_ROWS, D_MODEL), F32)
    for c in range(D_FF // FF_COLS):
        cols = slice(c * FF_COLS, (c + 1) * FF_COLS)
        gcols = slice(D_FF + c * FF_COLS, D_FF + (c + 1) * FF_COLS)
        val = jnp.dot(h_ref[...], wup_ref[:, cols], preferred_element_type=F32)
        gate_ref[halo:halo + TM_ROWS, cols] = jnp.dot(
            h_ref[...], wup_ref[:, gcols], preferred_element_type=F32)
        conv = cb_ref[:, cols]
        for tap in range(CONV_W):
            conv = conv + cw_ref[tap:tap + 1, cols] * gate_ref[tap * BATCH:tap * BATCH + TM_ROWS, cols]
        act = (conv * _sigmoid(conv) * val).astype(BF16)
        acc = acc + jnp.dot(act, wdn_ref[cols, :], preferred_element_type=F32)

    gate_ref[0:halo, :] = gate_ref[TM_ROWS:TM_ROWS + halo, :]
    out = x + res_gate * acc.reshape(TM_STEPS, BATCH, D_MODEL)
    if final_norm:
        ms = jnp.mean(out * out, axis=-1, keepdims=True)
        out = out * lax.rsqrt(ms + EPS) * fg_ref[...][None]
    o_ref[...] = out


def _conv_ffn(xt, mod, w_up, conv_w, conv_b, w_down, final_g, final_norm):
    tile = pl.BlockSpec((TM_STEPS, BATCH, D_MODEL), lambda i: (i, 0, 0))
    return pl.pallas_call(
        functools.partial(_ffn_kernel, final_norm=final_norm),
        out_shape=jax.ShapeDtypeStruct((SEQ, BATCH, D_MODEL), F32),
        grid=(SEQ // TM_STEPS,),
        in_specs=[
            tile,
            _resident((BATCH, 6 * D_MODEL)),
            _resident((D_MODEL, 2 * D_FF)),
            _resident((CONV_W, D_FF)),
            _resident((1, D_FF)),
            _resident((D_FF, D_MODEL)),
            _resident((1, D_MODEL)),
        ],
        out_specs=tile,
        scratch_shapes=[
            pltpu.VMEM((TM_ROWS, D_MODEL), BF16),
            pltpu.VMEM((TM_ROWS + (CONV_W - 1) * BATCH, D_FF), F32),
        ],
        compiler_params=_params(1),
        name="conv_ffn_final" if final_norm else "conv_ffn",
    )(xt, mod, w_up, conv_w.reshape(CONV_W, D_FF), conv_b.reshape(1, D_FF), w_down,
      final_g.reshape(1, D_MODEL))


def _s5_prep_kernel(lr_ref, li_ref, ldt_ref, br_ref, bi_ref, ar_ref, ai_ref, bbr_ref, bbi_ref):
    dt = jnp.exp(ldt_ref[...])
    lr = lr_ref[...]
    li = li_ref[...]
    mag = jnp.exp(lr * dt)
    ar = mag * jnp.cos(li * dt)
    ai = mag * jnp.sin(li * dt)
    nr = ar - 1.0
    den = lr * lr + li * li
    fr = (nr * lr + ai * li) / den
    fi = (ai * lr - nr * li) / den
    ar_ref[...] = ar
    ai_ref[...] = ai
    br = br_ref[...]
    bi = bi_ref[...]
    bbr_ref[...] = fr * br - fi * bi
    bbi_ref[...] = fr * bi + fi * br


def _s5_discretise(lam_re, lam_im, log_dt, b_re, b_im):
    g, p, n = S5_GROUPS, S5_STATE, S5_GROUP
    vec = jax.ShapeDtypeStruct((g, 1, p), F32)
    mat = jax.ShapeDtypeStruct((g, n, p), F32)
    return pl.pallas_call(
        _s5_prep_kernel,
        out_shape=(vec, vec, mat, mat),
        name="s5_discretise",
    )(lam_re.reshape(g, 1, p), lam_im.reshape(g, 1, p), log_dt.reshape(g, 1, 1),
      b_re.transpose(0, 2, 1), b_im.transpose(0, 2, 1))


def _block_diagonal(blocks):
    nb, ng, r, c = blocks.shape
    eye = jnp.eye(ng, dtype=blocks.dtype)
    out = blocks[:, :, :, None, :] * eye[None, :, None, :, None]
    return out.reshape(nb, ng * r, ng * c)


def _s5_kernel(x_ref, mod_ref, win_ref, bm_ref, are_ref, aim_ref, cm_ref, d_ref, wglu_ref, o_ref,
               state_ref, h_ref, u_ref, ub_ref, bu_ref, s_ref, y_ref):
    @pl.when(pl.program_id(0) == 0)
    def _():
        state_ref[...] = jnp.zeros_like(state_ref)

    x = x_ref[...]
    shift = mod_ref[:, 0:D_MODEL][None]
    scale = mod_ref[:, D_MODEL:2 * D_MODEL][None]
    res_gate = mod_ref[:, 2 * D_MODEL:3 * D_MODEL][None]
    h_ref[...] = _norm_mod(x, scale, shift).reshape(TM_ROWS, D_MODEL).astype(BF16)
    u_ref[...] = jnp.dot(h_ref[...], win_ref[...], preferred_element_type=F32)
    ub_ref[...] = u_ref[...].astype(BF16)

    ns = S5_BLOCK_STATES
    for cb in range(S5_BLOCKS):
        cols = slice(cb * LANES, (cb + 1) * LANES)
        bu_ref[...] = jnp.dot(ub_ref[:, cols], bm_ref[cb], preferred_element_type=F32)
        ar = are_ref[cb]
        ai = aim_ref[cb]

        def step(t, carry):
            sr, si = carry
            r0 = pl.multiple_of(t * BATCH, BATCH)
            new_r = ar * sr - ai * si + bu_ref[pl.ds(r0, BATCH), 0:ns]
            new_i = ar * si + ai * sr + bu_ref[pl.ds(r0, BATCH), ns:2 * ns]
            s_ref[pl.ds(r0, BATCH), 0:ns] = new_r
            s_ref[pl.ds(r0, BATCH), ns:2 * ns] = new_i
            return new_r, new_i

        sr, si = lax.fori_loop(0, TM_STEPS, step,
                               (state_ref[cb, :, 0:ns], state_ref[cb, :, ns:2 * ns]), unroll=4)
        state_ref[cb, :, 0:ns] = sr
        state_ref[cb, :, ns:2 * ns] = si
        y_ref[:, cols] = jnp.dot(s_ref[...].astype(BF16), cm_ref[cb], preferred_element_type=F32)

    y = y_ref[...] + d_ref[...] * u_ref[...]
    z = jnp.dot(jax.nn.gelu(y).astype(BF16), wglu_ref[...], preferred_element_type=F32)
    mixed = z[:, :D_MODEL] * _sigmoid(z[:, D_MODEL:])
    o_ref[...] = x + res_gate * mixed.reshape(TM_STEPS, BATCH, D_MODEL)


def _s5_mixer(xt, mod, w_in, lam_re, lam_im, log_dt, b_re, b_im, c_re, c_im, d_skip, w_glu):
    ar, ai, bbr, bbi = _s5_discretise(lam_re, lam_im, log_dt, b_re, b_im)
    ng = LANES // S5_GROUP
    nb = S5_BLOCKS
    b_mat = jnp.concatenate([
        _block_diagonal(bbr.reshape(nb, ng, S5_GROUP, S5_STATE)),
        _block_diagonal(bbi.reshape(nb, ng, S5_GROUP, S5_STATE)),
    ], axis=-1).astype(BF16)
    c_mat = jnp.concatenate([
        _block_diagonal(c_re.transpose(0, 2, 1).reshape(nb, ng, S5_STATE, S5_GROUP)),
        _block_diagonal(-c_im.transpose(0, 2, 1).reshape(nb, ng, S5_STATE, S5_GROUP)),
    ], axis=1).astype(BF16)
    a_re = jnp.broadcast_to(ar.reshape(nb, 1, S5_BLOCK_STATES), (nb, BATCH, S5_BLOCK_STATES))
    a_im = jnp.broadcast_to(ai.reshape(nb, 1, S5_BLOCK_STATES), (nb, BATCH, S5_BLOCK_STATES))

    tile = pl.BlockSpec((TM_STEPS, BATCH, D_MODEL), lambda i: (i, 0, 0))
    return pl.pallas_call(
        _s5_kernel,
        out_shape=jax.ShapeDtypeStruct((SEQ, BATCH, D_MODEL), F32),
        grid=(SEQ // TM_STEPS,),
        in_specs=[
            tile,
            _resident((BATCH, 6 * D_MODEL)),
            _resident((D_MODEL, D_MODEL)),
            _resident((nb, LANES, 2 * S5_BLOCK_STATES)),
            _resident((nb, BATCH, S5_BLOCK_STATES)),
            _resident((nb, BATCH, S5_BLOCK_STATES)),
            _resident((nb, 2 * S5_BLOCK_STATES, LANES)),
            _resident((1, D_MODEL)),
            _resident((D_MODEL, 2 * D_MODEL)),
        ],
        out_specs=tile,
        scratch_shapes=[
            pltpu.VMEM((nb, BATCH, 2 * S5_BLOCK_STATES), F32),
            pltpu.VMEM((TM_ROWS, D_MODEL), BF16),
            pltpu.VMEM((TM_ROWS, D_MODEL), F32),
            pltpu.VMEM((TM_ROWS, D_MODEL), BF16),
            pltpu.VMEM((TM_ROWS, 2 * S5_BLOCK_STATES), F32),
            pltpu.VMEM((TM_ROWS, 2 * S5_BLOCK_STATES), F32),
            pltpu.VMEM((TM_ROWS, D_MODEL), F32),
        ],
        compiler_params=_params(1),
        name="s5_mixer",
    )(xt, mod, w_in, b_mat, a_re, a_im, c_mat, d_skip.reshape(1, D_MODEL), w_glu)


def kernel(x, c, pos, ada_w, ada_b, ret_w_in, ret_w_out, s5_w_in, s5_lam_re, s5_lam_im, s5_log_dt,
           s5_b_re, s5_b_im, s5_c_re, s5_c_im, s5_d, s5_w_glu, ffn_w_up, ffn_conv_w, ffn_conv_b,
           ffn_w_down, final_norm_g):
    mod = _modulation(c, ada_w, ada_b)
    half = RET_DK // 2
    inv_freq = jnp.power(ROPE_BASE, -jnp.arange(half, dtype=F32) / half).reshape(1, half)

    q, k, v, sg = _ret_projection(x, pos, mod[0], inv_freq, ret_w_in[0].astype(BF16))
    x = _retention_core(q, k, v, sg, x, mod[0], ret_w_out[0].astype(BF16))

    xt = x.transpose(1, 0, 2)
    xt = _conv_ffn(xt, mod[0], ffn_w_up[0].astype(BF16), ffn_conv_w[0], ffn_conv_b[0],
                   ffn_w_down[0].astype(BF16), final_norm_g, final_norm=False)
    xt = _s5_mixer(xt, mod[1], s5_w_in[0].astype(BF16), s5_lam_re[0], s5_lam_im[0], s5_log_dt[0],
                   s5_b_re[0], s5_b_im[0], s5_c_re[0], s5_c_im[0], s5_d[0],
                   s5_w_glu[0].astype(BF16))
    xt = _conv_ffn(xt, mod[1], ffn_w_up[1].astype(BF16), ffn_conv_w[1], ffn_conv_b[1],
                   ffn_w_down[1].astype(BF16), final_norm_g, final_norm=True)
    return xt.transpose(1, 0, 2)
```

```python
import functools
import math

import jax
import jax.numpy as jnp
from jax import lax
from jax.experimental import pallas as pl
from jax.experimental.pallas import tpu as pltpu

F32 = jnp.float32
BF16 = jnp.bfloat16

D_MODEL = 1024
BATCH = 16
SEQ = 4096
DEPTH = 2
RET_HEADS = 4
RET_DK = 256
RET_DV = 512
RET_QK = RET_HEADS * RET_DK
RET_V = RET_HEADS * RET_DV
RET_PROJ = 2 * RET_QK + 2 * RET_V
RET_CHUNK = 256
ROPE_BASE = 10000.0
S5_GROUP = 16
S5_GROUPS = 64
S5_STATE = 64
D_FF = 2816
CONV_W = 3
EPS = 1e-6

LANES = 128
SUBLANES = 8
VMEM_LIMIT_BYTES = 56 * 1024 * 1024

PROJ_ROWS = 512
RET_ROWS = 512
TM_STEPS = 32
TM_ROWS = TM_STEPS * BATCH
FF_ROWS = 512
MXU_TILE = 256
FF_CHUNKS = ((0, 4 * MXU_TILE), (4 * MXU_TILE, 8 * MXU_TILE), (8 * MXU_TILE, D_FF))
S5_BLOCKS = D_MODEL // LANES
S5_BLOCK_STATES = (LANES // S5_GROUP) * S5_STATE
MOD_COLS = 1536


def _resident(shape):
    zeros = (0,) * len(shape)
    return pl.BlockSpec(shape, lambda *_: zeros, pipeline_mode=pl.Buffered(1))


def _params(n_axes):
    return pltpu.CompilerParams(
        dimension_semantics=("arbitrary",) * n_axes,
        vmem_limit_bytes=VMEM_LIMIT_BYTES,
    )


def _sigmoid(x):
    return jax.nn.sigmoid(x)


def _norm_mod(x, scale, shift):
    ms = jnp.mean(x * x, axis=-1, keepdims=True)
    return x * lax.rsqrt(ms + EPS) * (1.0 + scale) + shift


def _mod_kernel(c_ref, w_ref, b_ref, o_ref):
    c = c_ref[...]
    cond = c * _sigmoid(c)
    o_ref[0] = jnp.dot(cond, w_ref[0], preferred_element_type=F32,
                       precision=lax.Precision.HIGHEST) + b_ref[0]


def _modulation(c, ada_w, ada_b):
    n_cols = 6 * D_MODEL
    return pl.pallas_call(
        _mod_kernel,
        out_shape=jax.ShapeDtypeStruct((DEPTH, BATCH, n_cols), F32),
        grid=(DEPTH, n_cols // MOD_COLS),
        in_specs=[
            pl.BlockSpec((BATCH, D_MODEL), lambda i, j: (0, 0)),
            pl.BlockSpec((1, D_MODEL, MOD_COLS), lambda i, j: (i, 0, j)),
            pl.BlockSpec((1, 1, MOD_COLS), lambda i, j: (i, 0, j)),
        ],
        out_specs=pl.BlockSpec((1, BATCH, MOD_COLS), lambda i, j: (i, 0, j)),
        compiler_params=_params(2),
        name="adaln_modulation",
    )(c, ada_w, ada_b.reshape(DEPTH, 1, n_cols))


def _ret_proj_kernel(x_ref, pos_ref, mod_ref, invf_ref, w_ref,
                     q_ref, k_ref, v_ref, sg_ref, h_ref):
    shift = mod_ref[0, :, 0:D_MODEL]
    scale = mod_ref[0, :, D_MODEL:2 * D_MODEL]
    h_ref[...] = _norm_mod(x_ref[0], scale, shift).astype(BF16)

    ang = pos_ref[0].astype(F32) * invf_ref[...]
    cos = jnp.cos(ang)
    sin = jnp.sin(ang)
    k_scale = RET_DK ** -0.5
    cos_k = cos * k_scale
    sin_k = sin * k_scale
    half = RET_DK // 2

    for hd in range(RET_HEADS):
        for dst, base, cs, sn in ((q_ref, 0, cos, sin), (k_ref, RET_QK, cos_k, sin_k)):
            c0 = base + hd * RET_DK
            y = jnp.dot(h_ref[...], w_ref[:, c0:c0 + RET_DK], preferred_element_type=F32)
            t1 = y[:, :half]
            t2 = y[:, half:]
            o0 = hd * RET_DK
            dst[0, :, o0:o0 + half] = (t1 * cs - t2 * sn).astype(BF16)
            dst[0, :, o0 + half:o0 + RET_DK] = (t1 * sn + t2 * cs).astype(BF16)

    for hd in range(RET_HEADS):
        c0 = 2 * RET_QK + hd * RET_DV
        v_ref[0, :, hd * RET_DV:(hd + 1) * RET_DV] = jnp.dot(
            h_ref[...], w_ref[:, c0:c0 + RET_DV], preferred_element_type=F32).astype(BF16)
    for hd in range(RET_HEADS):
        c0 = 2 * RET_QK + RET_V + hd * RET_DV
        g = jnp.dot(h_ref[...], w_ref[:, c0:c0 + RET_DV], preferred_element_type=F32)
        sg_ref[0, :, hd * RET_DV:(hd + 1) * RET_DV] = (g * _sigmoid(g)).astype(BF16)


def _ret_projection(x, pos, mod0, inv_freq, w_in):
    rows = PROJ_ROWS
    row_map = lambda b, t: (b, t, 0)
    return pl.pallas_call(
        _ret_proj_kernel,
        out_shape=(
            jax.ShapeDtypeStruct((BATCH, SEQ, RET_QK), BF16),
            jax.ShapeDtypeStruct((BATCH, SEQ, RET_QK), BF16),
            jax.ShapeDtypeStruct((BATCH, SEQ, RET_V), BF16),
            jax.ShapeDtypeStruct((BATCH, SEQ, RET_V), BF16),
        ),
        grid=(BATCH, SEQ // rows),
        in_specs=[
            pl.BlockSpec((1, rows, D_MODEL), row_map),
            pl.BlockSpec((1, rows, 1), row_map),
            pl.BlockSpec((1, 1, 6 * D_MODEL), lambda b, t: (b, 0, 0)),
            _resident((1, RET_DK // 2)),
            _resident((D_MODEL, RET_PROJ)),
        ],
        out_specs=(
            pl.BlockSpec((1, rows, RET_QK), row_map),
            pl.BlockSpec((1, rows, RET_QK), row_map),
            pl.BlockSpec((1, rows, RET_V), row_map),
            pl.BlockSpec((1, rows, RET_V), row_map),
        ),
        scratch_shapes=[pltpu.VMEM((rows, D_MODEL), BF16)],
        compiler_params=_params(2),
        name="retention_projection",
    )(x, pos.reshape(BATCH, SEQ, 1), mod0.reshape(BATCH, 1, 6 * D_MODEL), inv_freq, w_in)


def _ret_core_kernel(q_ref, k_ref, v_ref, sg_ref, x_ref, mod_ref, intra_ref, cross_ref,
                     tost_ref, decay_ref, wout_ref, o_ref, state_ref, go_ref):
    @pl.when(pl.program_id(1) == 0)
    def _():
        state_ref[...] = jnp.zeros_like(state_ref)

    contract_last = (((1,), (1,)), ((), ()))
    contract_first = (((0,), (0,)), ((), ()))
    for hd in range(RET_HEADS):
        qk_cols = slice(hd * RET_DK, (hd + 1) * RET_DK)
        v_cols = slice(hd * RET_DV, (hd + 1) * RET_DV)
        for c in range(RET_ROWS // RET_CHUNK):
            rows = slice(c * RET_CHUNK, (c + 1) * RET_CHUNK)
            qc = q_ref[0, rows, qk_cols]
            kc = k_ref[0, rows, qk_cols]
            vc = v_ref[0, rows, v_cols]
            state = state_ref[hd]
            s = lax.dot_general(qc, kc, contract_last, preferred_element_type=F32) * intra_ref[hd]
            o = jnp.dot(s.astype(BF16), vc, preferred_element_type=F32)
            o = o + jnp.dot(qc, state.astype(BF16), preferred_element_type=F32) * cross_ref[hd]
            kd = (kc.astype(F32) * tost_ref[hd]).astype(BF16)
            state_ref[hd] = decay_ref[hd] * state + lax.dot_general(
                kd, vc, contract_first, preferred_element_type=F32)
            mu = jnp.mean(o, axis=-1, keepdims=True)
            oc = o - mu
            on = oc * lax.rsqrt(jnp.mean(oc * oc, axis=-1, keepdims=True) + EPS)
            go_ref[rows, v_cols] = (sg_ref[0, rows, v_cols].astype(F32) * on).astype(BF16)

    y = jnp.dot(go_ref[...], wout_ref[...], preferred_element_type=F32)
    gate = mod_ref[0, :, 2 * D_MODEL:3 * D_MODEL]
    o_ref[...] = x_ref[0] + gate * y


def _retention_core(q, k, v, sg, x, mod0, w_out):
    log_gamma = jnp.log1p(-jnp.exp2(-5.0 - jnp.arange(RET_HEADS, dtype=F32)))
    idx = jnp.arange(RET_CHUNK, dtype=F32)
    diff = idx[:, None] - idx[None, :]
    intra = jnp.where(diff[None] >= 0,
                      jnp.exp(log_gamma[:, None, None] * jnp.maximum(diff, 0.0)[None]), 0.0)
    cross = jnp.exp(log_gamma[:, None] * (idx + 1.0))[:, :, None]
    to_state = jnp.exp(log_gamma[:, None] * (RET_CHUNK - 1.0 - idx))[:, :, None]
    chunk_decay = jnp.exp(log_gamma * RET_CHUNK)

    rows = RET_ROWS
    row_map = lambda b, t: (b, t, 0)
    return pl.pallas_call(
        _ret_core_kernel,
        out_shape=jax.ShapeDtypeStruct((SEQ, BATCH * D_MODEL), F32),
        grid=(BATCH, SEQ // rows),
        in_specs=[
            pl.BlockSpec((1, rows, RET_QK), row_map),
            pl.BlockSpec((1, rows, RET_QK), row_map),
            pl.BlockSpec((1, rows, RET_V), row_map),
            pl.BlockSpec((1, rows, RET_V), row_map),
            pl.BlockSpec((1, rows, D_MODEL), row_map),
            pl.BlockSpec((1, 1, 6 * D_MODEL), lambda b, t: (b, 0, 0)),
            _resident((RET_HEADS, RET_CHUNK, RET_CHUNK)),
            _resident((RET_HEADS, RET_CHUNK, 1)),
            _resident((RET_HEADS, RET_CHUNK, 1)),
            pl.BlockSpec(memory_space=pltpu.SMEM),
            _resident((RET_V, D_MODEL)),
        ],
        out_specs=pl.BlockSpec((rows, D_MODEL), lambda b, t: (t, b)),
        scratch_shapes=[
            pltpu.VMEM((RET_HEADS, RET_DK, RET_DV), F32),
            pltpu.VMEM((rows, RET_V), BF16),
        ],
        compiler_params=_params(2),
        name="retention_core",
    )(q, k, v, sg, x, mod0.reshape(BATCH, 1, 6 * D_MODEL), intra, cross, to_state,
      chunk_decay, w_out)


FF_HALO = SUBLANES


def _ffn_kernel(x_ref, mod_ref, wup_ref, cw_ref, cb_ref, wdn_ref, fg_ref, o_ref,
                h_ref, gate_ref, *, last):
    rows = FF_ROWS

    @pl.when(pl.program_id(1) == 0)
    def _():
        gate_ref[0:FF_HALO, :] = jnp.zeros((FF_HALO, D_FF), F32)

    x = x_ref[...]
    shift = mod_ref[0, :, 3 * D_MODEL:4 * D_MODEL]
    scale = mod_ref[0, :, 4 * D_MODEL:5 * D_MODEL]
    res_gate = mod_ref[0, :, 5 * D_MODEL:6 * D_MODEL]
    h_ref[...] = _norm_mod(x, scale, shift).astype(BF16)

    acc = None
    for c0, c1 in FF_CHUNKS:
        val = jnp.dot(h_ref[...], wup_ref[:, c0:c1], preferred_element_type=F32)
        gate_ref[FF_HALO:FF_HALO + rows, c0:c1] = jnp.dot(
            h_ref[...], wup_ref[:, D_FF + c0:D_FF + c1], preferred_element_type=F32)
        conv = cb_ref[:, c0:c1]
        for tap in range(CONV_W):
            r0 = FF_HALO - (CONV_W - 1) + tap
            conv = conv + cw_ref[tap:tap + 1, c0:c1] * gate_ref[r0:r0 + rows, c0:c1]
        act = (conv * _sigmoid(conv) * val).astype(BF16)
        part = jnp.dot(act, wdn_ref[c0:c1, :], preferred_element_type=F32)
        acc = part if acc is None else acc + part

    gate_ref[0:FF_HALO, :] = gate_ref[rows:rows + FF_HALO, :]
    out = x + res_gate * acc
    if last:
        ms = jnp.mean(out * out, axis=-1, keepdims=True)
        out = out * lax.rsqrt(ms + EPS) * fg_ref[...]
    o_ref[...] = out


def _conv_ffn(xt, mod, w_up, conv_w, conv_b, w_down, final_g, last):
    rows = FF_ROWS
    steps = SEQ // rows
    time_major = pl.BlockSpec((rows, D_MODEL), lambda b, t: (t, b))
    if last:
        out_shape = jax.ShapeDtypeStruct((BATCH * SEQ, D_MODEL), F32)
        out_spec = pl.BlockSpec((rows, D_MODEL), lambda b, t: (b * steps + t, 0))
    else:
        out_shape = jax.ShapeDtypeStruct((SEQ, BATCH * D_MODEL), F32)
        out_spec = time_major
    return pl.pallas_call(
        functools.partial(_ffn_kernel, last=last),
        out_shape=out_shape,
        grid=(BATCH, steps),
        in_specs=[
            time_major,
            pl.BlockSpec((1, 1, 6 * D_MODEL), lambda b, t: (b, 0, 0)),
            _resident((D_MODEL, 2 * D_FF)),
            _resident((CONV_W, D_FF)),
            _resident((1, D_FF)),
            _resident((D_FF, D_MODEL)),
            _resident((1, D_MODEL)),
        ],
        out_specs=out_spec,
        scratch_shapes=[
            pltpu.VMEM((rows, D_MODEL), BF16),
            pltpu.VMEM((FF_HALO + rows, D_FF), F32),
        ],
        compiler_params=_params(2),
        name="conv_ffn_last" if last else "conv_ffn",
    )(xt, mod.reshape(BATCH, 1, 6 * D_MODEL), w_up, conv_w.reshape(CONV_W, D_FF),
      conv_b.reshape(1, D_FF), w_down, final_g.reshape(1, D_MODEL))


def _s5_prep_kernel(lr_ref, li_ref, ldt_ref, br_ref, bi_ref, ar_ref, ai_ref, bbr_ref, bbi_ref):
    dt = jnp.exp(ldt_ref[...])
    lr = lr_ref[...]
    li = li_ref[...]
    mag = jnp.exp(lr * dt)
    ar = mag * jnp.cos(li * dt)
    ai = mag * jnp.sin(li * dt)
    nr = ar - 1.0
    den = lr * lr + li * li
    fr = (nr * lr + ai * li) / den
    fi = (ai * lr - nr * li) / den
    ar_ref[...] = ar
    ai_ref[...] = ai
    br = br_ref[...]
    bi = bi_ref[...]
    bbr_ref[...] = fr * br - fi * bi
    bbi_ref[...] = fr * bi + fi * br


def _s5_discretise(lam_re, lam_im, log_dt, b_re, b_im):
    g, p, n = S5_GROUPS, S5_STATE, S5_GROUP
    vec = jax.ShapeDtypeStruct((g, 1, p), F32)
    mat = jax.ShapeDtypeStruct((g, n, p), F32)
    return pl.pallas_call(
        _s5_prep_kernel,
        out_shape=(vec, vec, mat, mat),
        name="s5_discretise",
    )(lam_re.reshape(g, 1, p), lam_im.reshape(g, 1, p), log_dt.reshape(g, 1, 1),
      b_re.transpose(0, 2, 1), b_im.transpose(0, 2, 1))


def _block_diagonal(blocks):
    nb, ng, r, c = blocks.shape
    eye = jnp.eye(ng, dtype=blocks.dtype)
    out = blocks[:, :, :, None, :] * eye[None, :, None, :, None]
    return out.reshape(nb, ng * r, ng * c)


def _s5_kernel(x_ref, mod_ref, win_ref, bm_ref, are_ref, aim_ref, cm_ref, d_ref, wglu_ref, o_ref,
               state_ref, h_ref, u_ref, ub_ref, bu_ref, s_ref, y_ref):
    @pl.when(pl.program_id(0) == 0)
    def _():
        state_ref[...] = jnp.zeros_like(state_ref)

    x = x_ref[...]
    shift = mod_ref[:, 0:D_MODEL][None]
    scale = mod_ref[:, D_MODEL:2 * D_MODEL][None]
    res_gate = mod_ref[:, 2 * D_MODEL:3 * D_MODEL][None]
    h_ref[...] = _norm_mod(x, scale, shift).reshape(TM_ROWS, D_MODEL).astype(BF16)
    u_ref[...] = jnp.dot(h_ref[...], win_ref[...], preferred_element_type=F32)
    ub_ref[...] = u_ref[...].astype(BF16)

    ns = S5_BLOCK_STATES
    for cb in range(S5_BLOCKS):
        cols = slice(cb * LANES, (cb + 1) * LANES)
        slot = cb % 2
        bu_ref[slot] = jnp.dot(ub_ref[:, cols], bm_ref[cb], preferred_element_type=F32)
        ar = are_ref[cb]
        ai = aim_ref[cb]
        sr = state_ref[cb, :, 0:ns]
        si = state_ref[cb, :, ns:2 * ns]
        for t in range(TM_STEPS):
            rows = slice(t * BATCH, (t + 1) * BATCH)
            sr, si = (ar * sr - ai * si + bu_ref[slot, rows, 0:ns],
                      ar * si + ai * sr + bu_ref[slot, rows, ns:2 * ns])
            s_ref[slot, rows, 0:ns] = sr.astype(BF16)
            s_ref[slot, rows, ns:2 * ns] = si.astype(BF16)
        state_ref[cb, :, 0:ns] = sr
        state_ref[cb, :, ns:2 * ns] = si
        y_ref[:, cols] = jnp.dot(s_ref[slot], cm_ref[cb], preferred_element_type=F32)

    y = y_ref[...] + d_ref[...] * u_ref[...]
    z = jnp.dot(jax.nn.gelu(y).astype(BF16), wglu_ref[...], preferred_element_type=F32)
    mixed = z[:, :D_MODEL] * _sigmoid(z[:, D_MODEL:])
    o_ref[...] = x + res_gate * mixed.reshape(TM_STEPS, BATCH, D_MODEL)


def _s5_mixer(xt, mod, w_in, lam_re, lam_im, log_dt, b_re, b_im, c_re, c_im, d_skip, w_glu):
    ar, ai, bbr, bbi = _s5_discretise(lam_re, lam_im, log_dt, b_re, b_im)
    ng = LANES // S5_GROUP
    nb = S5_BLOCKS
    b_mat = jnp.concatenate([
        _block_diagonal(bbr.reshape(nb, ng, S5_GROUP, S5_STATE)),
        _block_diagonal(bbi.reshape(nb, ng, S5_GROUP, S5_STATE)),
    ], axis=-1).astype(BF16)
    c_mat = jnp.concatenate([
        _block_diagonal(c_re.transpose(0, 2, 1).reshape(nb, ng, S5_STATE, S5_GROUP)),
        _block_diagonal(-c_im.transpose(0, 2, 1).reshape(nb, ng, S5_STATE, S5_GROUP)),
    ], axis=1).astype(BF16)
    a_re = jnp.broadcast_to(ar.reshape(nb, 1, S5_BLOCK_STATES), (nb, BATCH, S5_BLOCK_STATES))
    a_im = jnp.broadcast_to(ai.reshape(nb, 1, S5_BLOCK_STATES), (nb, BATCH, S5_BLOCK_STATES))

    tile = pl.BlockSpec((TM_STEPS, BATCH, D_MODEL), lambda i: (i, 0, 0))
    out = pl.pallas_call(
        _s5_kernel,
        out_shape=jax.ShapeDtypeStruct((SEQ, BATCH, D_MODEL), F32),
        grid=(SEQ // TM_STEPS,),
        in_specs=[
            tile,
            _resident((BATCH, 6 * D_MODEL)),
            _resident((D_MODEL, D_MODEL)),
            _resident((nb, LANES, 2 * S5_BLOCK_STATES)),
            _resident((nb, BATCH, S5_BLOCK_STATES)),
            _resident((nb, BATCH, S5_BLOCK_STATES)),
            _resident((nb, 2 * S5_BLOCK_STATES, LANES)),
            _resident((1, D_MODEL)),
            _resident((D_MODEL, 2 * D_MODEL)),
        ],
        out_specs=tile,
        scratch_shapes=[
            pltpu.VMEM((nb, BATCH, 2 * S5_BLOCK_STATES), F32),
            pltpu.VMEM((TM_ROWS, D_MODEL), BF16),
            pltpu.VMEM((TM_ROWS, D_MODEL), F32),
            pltpu.VMEM((TM_ROWS, D_MODEL), BF16),
            pltpu.VMEM((2, TM_ROWS, 2 * S5_BLOCK_STATES), F32),
            pltpu.VMEM((2, TM_ROWS, 2 * S5_BLOCK_STATES), BF16),
            pltpu.VMEM((TM_ROWS, D_MODEL), F32),
        ],
        compiler_params=_params(1),
        name="s5_mixer",
    )(xt.reshape(SEQ, BATCH, D_MODEL), mod, w_in, b_mat, a_re, a_im, c_mat,
      d_skip.reshape(1, D_MODEL), w_glu)
    return out.reshape(SEQ, BATCH * D_MODEL)


def kernel(x, c, pos, ada_w, ada_b, ret_w_in, ret_w_out, s5_w_in, s5_lam_re, s5_lam_im, s5_log_dt,
           s5_b_re, s5_b_im, s5_c_re, s5_c_im, s5_d, s5_w_glu, ffn_w_up, ffn_conv_w, ffn_conv_b,
           ffn_w_down, final_norm_g):
    mod = _modulation(c, ada_w, ada_b)
    half = RET_DK // 2
    inv_freq = jnp.power(ROPE_BASE, -jnp.arange(half, dtype=F32) / half).reshape(1, half)

    q, k, v, sg = _ret_projection(x, pos, mod[0], inv_freq, ret_w_in[0].astype(BF16))
    xt = _retention_core(q, k, v, sg, x, mod[0], ret_w_out[0].astype(BF16))
    xt = _conv_ffn(xt, mod[0], ffn_w_up[0].astype(BF16), ffn_conv_w[0], ffn_conv_b[0],
                   ffn_w_down[0].astype(BF16), final_norm_g, last=False)
    xt = _s5_mixer(xt, mod[1], s5_w_in[0].astype(BF16), s5_lam_re[0], s5_lam_im[0], s5_log_dt[0],
                   s5_b_re[0], s5_b_im[0], s5_c_re[0], s5_c_im[0], s5_d[0],
                   s5_w_glu[0].astype(BF16))
    out = _conv_ffn(xt, mod[1], ffn_w_up[1].astype(BF16), ffn_conv_w[1], ffn_conv_b[1],
                    ffn_w_down[1].astype(BF16), final_norm_g, last=True)
    return out.reshape(BATCH, SEQ, D_MODEL)
```

```python
import functools
import math

import jax
import jax.numpy as jnp
from jax import lax
from jax.experimental import pallas as pl
from jax.experimental.pallas import tpu as pltpu

F32 = jnp.float32
BF16 = jnp.bfloat16

D_MODEL = 1024
BATCH = 16
SEQ = 4096
DEPTH = 2
RET_HEADS = 4
RET_DK = 256
RET_DV = 512
RET_QK = RET_HEADS * RET_DK
RET_V = RET_HEADS * RET_DV
RET_PROJ = 2 * RET_QK + 2 * RET_V
RET_CHUNK = 256
ROPE_BASE = 10000.0
S5_GROUP = 16
S5_GROUPS = 64
S5_STATE = 64
D_FF = 2816
CONV_W = 3
EPS = 1e-6

LANES = 128
SUBLANES = 8
VMEM_LIMIT_BYTES = 56 * 1024 * 1024

PROJ_ROWS = 512
RET_ROWS = 512
TM_STEPS = 32
TM_ROWS = TM_STEPS * BATCH
MXU_TILE = 256
FF_CHUNK = 2 * MXU_TILE
FF_CHUNKS = tuple((c, min(c + FF_CHUNK, D_FF)) for c in range(0, D_FF, FF_CHUNK))
CONV_HALO = (CONV_W - 1) * BATCH
S5_BLOCKS = D_MODEL // LANES
S5_BLOCK_STATES = (LANES // S5_GROUP) * S5_STATE
S5_LOOKAHEAD = 3
MOD_COLS = 1536


def _resident(shape):
    zeros = (0,) * len(shape)
    return pl.BlockSpec(shape, lambda *_: zeros, pipeline_mode=pl.Buffered(1))


def _params(n_axes):
    return pltpu.CompilerParams(
        dimension_semantics=("arbitrary",) * n_axes,
        vmem_limit_bytes=VMEM_LIMIT_BYTES,
    )


def _sigmoid(x):
    return jax.nn.sigmoid(x)


def _tile_in_copy(x_hbm, xbuf, sem, step, slot, b):
    src = x_hbm.at[pl.ds(step * TM_STEPS, TM_STEPS), pl.ds(b * D_MODEL, D_MODEL)]
    return pltpu.make_async_copy(src, xbuf.at[slot, :, b, :], sem.at[slot])


def _tile_out_copy(o_hbm, obuf, sem, step, slot, b, batch_major):
    if batch_major:
        dst = o_hbm.at[b, pl.ds(step * TM_STEPS, TM_STEPS), :]
    else:
        dst = o_hbm.at[pl.ds(step * TM_STEPS, TM_STEPS), pl.ds(b * D_MODEL, D_MODEL)]
    return pltpu.make_async_copy(obuf.at[slot, :, b, :], dst, sem.at[slot])


def _fetch_tile(x_hbm, xbuf, sem):
    i = pl.program_id(0)
    slot = lax.rem(i, 2)

    @pl.when(i == 0)
    def _():
        for b in range(BATCH):
            _tile_in_copy(x_hbm, xbuf, sem, 0, 0, b).start()

    @pl.when(i + 1 < pl.num_programs(0))
    def _():
        for b in range(BATCH):
            _tile_in_copy(x_hbm, xbuf, sem, i + 1, 1 - slot, b).start()

    for b in range(BATCH):
        _tile_in_copy(x_hbm, xbuf, sem, i, slot, b).wait()
    return slot


def _emit_tile(o_hbm, obuf, sem, slot, batch_major):
    i = pl.program_id(0)
    for b in range(BATCH):
        _tile_out_copy(o_hbm, obuf, sem, i, slot, b, batch_major).start()

    @pl.when(i >= 1)
    def _():
        for b in range(BATCH):
            _tile_out_copy(o_hbm, obuf, sem, i - 1, 1 - slot, b, batch_major).wait()

    @pl.when(i == pl.num_programs(0) - 1)
    def _():
        for b in range(BATCH):
            _tile_out_copy(o_hbm, obuf, sem, i, slot, b, batch_major).wait()


_TILE_SCRATCH = [
    pltpu.VMEM((2, TM_STEPS, BATCH, D_MODEL), F32),
    pltpu.VMEM((2, TM_STEPS, BATCH, D_MODEL), F32),
    pltpu.SemaphoreType.DMA((2,)),
    pltpu.SemaphoreType.DMA((2,)),
]


def _norm_mod(x, scale, shift):
    ms = jnp.mean(x * x, axis=-1, keepdims=True)
    return x * lax.rsqrt(ms + EPS) * (1.0 + scale) + shift


def _mod_kernel(c_ref, w_ref, b_ref, o_ref):
    c = c_ref[...]
    cond = c * _sigmoid(c)
    o_ref[0] = jnp.dot(cond, w_ref[0], preferred_element_type=F32,
                       precision=lax.Precision.HIGHEST) + b_ref[0]


def _modulation(c, ada_w, ada_b):
    n_cols = 6 * D_MODEL
    return pl.pallas_call(
        _mod_kernel,
        out_shape=jax.ShapeDtypeStruct((DEPTH, BATCH, n_cols), F32),
        grid=(DEPTH, n_cols // MOD_COLS),
        in_specs=[
            pl.BlockSpec((BATCH, D_MODEL), lambda i, j: (0, 0)),
            pl.BlockSpec((1, D_MODEL, MOD_COLS), lambda i, j: (i, 0, j)),
            pl.BlockSpec((1, 1, MOD_COLS), lambda i, j: (i, 0, j)),
        ],
        out_specs=pl.BlockSpec((1, BATCH, MOD_COLS), lambda i, j: (i, 0, j)),
        compiler_params=_params(2),
        name="adaln_modulation",
    )(c, ada_w, ada_b.reshape(DEPTH, 1, n_cols))


def _ret_proj_kernel(x_ref, pos_ref, mod_ref, invf_ref, w_ref,
                     q_ref, k_ref, v_ref, sg_ref, h_ref):
    shift = mod_ref[0, :, 0:D_MODEL]
    scale = mod_ref[0, :, D_MODEL:2 * D_MODEL]
    h_ref[...] = _norm_mod(x_ref[0], scale, shift).astype(BF16)

    for hd in range(RET_HEADS):
        c0 = 2 * RET_QK + hd * RET_DV
        v_ref[0, :, hd * RET_DV:(hd + 1) * RET_DV] = jnp.dot(
            h_ref[...], w_ref[:, c0:c0 + RET_DV], preferred_element_type=F32).astype(BF16)
    for hd in range(RET_HEADS):
        c0 = 2 * RET_QK + RET_V + hd * RET_DV
        g = jnp.dot(h_ref[...], w_ref[:, c0:c0 + RET_DV], preferred_element_type=F32)
        sg_ref[0, :, hd * RET_DV:(hd + 1) * RET_DV] = (g * _sigmoid(g)).astype(BF16)

    ang = pos_ref[0].astype(F32) * invf_ref[...]
    cos = jnp.cos(ang)
    sin = jnp.sin(ang)
    k_scale = RET_DK ** -0.5
    cos_k = cos * k_scale
    sin_k = sin * k_scale
    half = RET_DK // 2

    for hd in range(RET_HEADS):
        for dst, base, cs, sn in ((q_ref, 0, cos, sin), (k_ref, RET_QK, cos_k, sin_k)):
            c0 = base + hd * RET_DK
            y = jnp.dot(h_ref[...], w_ref[:, c0:c0 + RET_DK], preferred_element_type=F32)
            t1 = y[:, :half]
            t2 = y[:, half:]
            o0 = hd * RET_DK
            dst[0, :, o0:o0 + half] = (t1 * cs - t2 * sn).astype(BF16)
            dst[0, :, o0 + half:o0 + RET_DK] = (t1 * sn + t2 * cs).astype(BF16)


def _ret_projection(x, pos, mod0, inv_freq, w_in):
    rows = PROJ_ROWS
    row_map = lambda b, t: (b, t, 0)
    return pl.pallas_call(
        _ret_proj_kernel,
        out_shape=(
            jax.ShapeDtypeStruct((BATCH, SEQ, RET_QK), BF16),
            jax.ShapeDtypeStruct((BATCH, SEQ, RET_QK), BF16),
            jax.ShapeDtypeStruct((BATCH, SEQ, RET_V), BF16),
            jax.ShapeDtypeStruct((BATCH, SEQ, RET_V), BF16),
        ),
        grid=(BATCH, SEQ // rows),
        in_specs=[
            pl.BlockSpec((1, rows, D_MODEL), row_map),
            pl.BlockSpec((1, rows, 1), row_map),
            pl.BlockSpec((1, 1, 6 * D_MODEL), lambda b, t: (b, 0, 0)),
            _resident((1, RET_DK // 2)),
            _resident((D_MODEL, RET_PROJ)),
        ],
        out_specs=(
            pl.BlockSpec((1, rows, RET_QK), row_map),
            pl.BlockSpec((1, rows, RET_QK), row_map),
            pl.BlockSpec((1, rows, RET_V), row_map),
            pl.BlockSpec((1, rows, RET_V), row_map),
        ),
        scratch_shapes=[pltpu.VMEM((rows, D_MODEL), BF16)],
        compiler_params=_params(2),
        name="retention_projection",
    )(x, pos.reshape(BATCH, SEQ, 1), mod0.reshape(BATCH, 1, 6 * D_MODEL), inv_freq, w_in)


def _ret_core_kernel(q_ref, k_ref, v_ref, sg_ref, x_ref, mod_ref, intra_ref, cross_ref,
                     tost_ref, decay_ref, wout_ref, o_ref, state_ref, go_ref):
    @pl.when(pl.program_id(1) == 0)
    def _():
        state_ref[...] = jnp.zeros_like(state_ref)

    contract_last = (((1,), (1,)), ((), ()))
    contract_first = (((0,), (0,)), ((), ()))
    for c in range(RET_ROWS // RET_CHUNK):
        rows = slice(c * RET_CHUNK, (c + 1) * RET_CHUNK)
        for hd in range(RET_HEADS):
            qk_cols = slice(hd * RET_DK, (hd + 1) * RET_DK)
            v_cols = slice(hd * RET_DV, (hd + 1) * RET_DV)
            qc = q_ref[0, rows, qk_cols]
            kc = k_ref[0, rows, qk_cols]
            vc = v_ref[0, rows, v_cols]
            state = state_ref[hd]
            s = lax.dot_general(qc, kc, contract_last, preferred_element_type=F32) * intra_ref[hd]
            o = jnp.dot(s.astype(BF16), vc, preferred_element_type=F32)
            o = o + jnp.dot(qc, state.astype(BF16), preferred_element_type=F32) * cross_ref[hd]
            kd = (kc.astype(F32) * tost_ref[hd]).astype(BF16)
            state_ref[hd] = decay_ref[hd] * state + lax.dot_general(
                kd, vc, contract_first, preferred_element_type=F32)
            mu = jnp.mean(o, axis=-1, keepdims=True)
            oc = o - mu
            on = oc * lax.rsqrt(jnp.mean(oc * oc, axis=-1, keepdims=True) + EPS)
            go_ref[rows, v_cols] = (sg_ref[0, rows, v_cols].astype(F32) * on).astype(BF16)

    y = jnp.dot(go_ref[...], wout_ref[...], preferred_element_type=F32)
    gate = mod_ref[0, :, 2 * D_MODEL:3 * D_MODEL]
    o_ref[...] = x_ref[0] + gate * y


def _retention_core(q, k, v, sg, x, mod0, w_out):
    log_gamma = jnp.log1p(-jnp.exp2(-5.0 - jnp.arange(RET_HEADS, dtype=F32)))
    idx = jnp.arange(RET_CHUNK, dtype=F32)
    diff = idx[:, None] - idx[None, :]
    intra = jnp.where(diff[None] >= 0,
                      jnp.exp(log_gamma[:, None, None] * jnp.maximum(diff, 0.0)[None]), 0.0)
    cross = jnp.exp(log_gamma[:, None] * (idx + 1.0))[:, :, None]
    to_state = jnp.exp(log_gamma[:, None] * (RET_CHUNK - 1.0 - idx))[:, :, None]
    chunk_decay = jnp.exp(log_gamma * RET_CHUNK)

    rows = RET_ROWS
    row_map = lambda b, t: (b, t, 0)
    return pl.pallas_call(
        _ret_core_kernel,
        out_shape=jax.ShapeDtypeStruct((SEQ, BATCH * D_MODEL), F32),
        grid=(BATCH, SEQ // rows),
        in_specs=[
            pl.BlockSpec((1, rows, RET_QK), row_map),
            pl.BlockSpec((1, rows, RET_QK), row_map),
            pl.BlockSpec((1, rows, RET_V), row_map),
            pl.BlockSpec((1, rows, RET_V), row_map),
            pl.BlockSpec((1, rows, D_MODEL), row_map),
            pl.BlockSpec((1, 1, 6 * D_MODEL), lambda b, t: (b, 0, 0)),
            _resident((RET_HEADS, RET_CHUNK, RET_CHUNK)),
            _resident((RET_HEADS, RET_CHUNK, 1)),
            _resident((RET_HEADS, RET_CHUNK, 1)),
            pl.BlockSpec(memory_space=pltpu.SMEM),
            _resident((RET_V, D_MODEL)),
        ],
        out_specs=pl.BlockSpec((rows, D_MODEL), lambda b, t: (t, b)),
        scratch_shapes=[
            pltpu.VMEM((RET_HEADS, RET_DK, RET_DV), F32),
            pltpu.VMEM((rows, RET_V), BF16),
        ],
        compiler_params=_params(2),
        name="retention_core",
    )(q, k, v, sg, x, mod0.reshape(BATCH, 1, 6 * D_MODEL), intra, cross, to_state,
      chunk_decay, w_out)


def _ffn_kernel(x_hbm, mod_ref, wup_ref, cw_ref, cb_ref, wdn_ref, fg_ref, o_hbm,
                xbuf, obuf, sem_in, sem_out, h_ref, gate_ref, *, last):
    slot = _fetch_tile(x_hbm, xbuf, sem_in)

    @pl.when(pl.program_id(0) == 0)
    def _():
        gate_ref[0:CONV_HALO, :] = jnp.zeros((CONV_HALO, D_FF), F32)

    x = xbuf[slot]
    shift = mod_ref[:, 3 * D_MODEL:4 * D_MODEL][None]
    scale = mod_ref[:, 4 * D_MODEL:5 * D_MODEL][None]
    res_gate = mod_ref[:, 5 * D_MODEL:6 * D_MODEL][None]
    h_ref[...] = _norm_mod(x, scale, shift).reshape(TM_ROWS, D_MODEL).astype(BF16)

    def up_projection(c0, c1):
        gate_ref[CONV_HALO:CONV_HALO + TM_ROWS, c0:c1] = jnp.dot(
            h_ref[...], wup_ref[:, D_FF + c0:D_FF + c1], preferred_element_type=F32)
        return jnp.dot(h_ref[...], wup_ref[:, c0:c1], preferred_element_type=F32)

    acc = None
    val_next = up_projection(*FF_CHUNKS[0])
    for idx, (c0, c1) in enumerate(FF_CHUNKS):
        val = val_next
        if idx + 1 < len(FF_CHUNKS):
            val_next = up_projection(*FF_CHUNKS[idx + 1])
        conv = cb_ref[:, c0:c1]
        for tap in range(CONV_W):
            conv = conv + cw_ref[tap:tap + 1, c0:c1] * gate_ref[tap * BATCH:tap * BATCH + TM_ROWS, c0:c1]
        act = (conv * _sigmoid(conv) * val).astype(BF16)
        part = jnp.dot(act, wdn_ref[c0:c1, :], preferred_element_type=F32)
        acc = part if acc is None else acc + part

    gate_ref[0:CONV_HALO, :] = gate_ref[TM_ROWS:TM_ROWS + CONV_HALO, :]
    out = x + res_gate * acc.reshape(TM_STEPS, BATCH, D_MODEL)
    if last:
        ms = jnp.mean(out * out, axis=-1, keepdims=True)
        out = out * lax.rsqrt(ms + EPS) * fg_ref[...][None]
    obuf[slot] = out
    _emit_tile(o_hbm, obuf, sem_out, slot, batch_major=last)


def _conv_ffn(xt, mod, w_up, conv_w, conv_b, w_down, final_g, last):
    if last:
        out_shape = jax.ShapeDtypeStruct((BATCH, SEQ, D_MODEL), F32)
    else:
        out_shape = jax.ShapeDtypeStruct((SEQ, BATCH * D_MODEL), F32)
    return pl.pallas_call(
        functools.partial(_ffn_kernel, last=last),
        out_shape=out_shape,
        grid=(SEQ // TM_STEPS,),
        in_specs=[
            pl.BlockSpec(memory_space=pl.ANY),
            _resident((BATCH, 6 * D_MODEL)),
            _resident((D_MODEL, 2 * D_FF)),
            _resident((CONV_W, D_FF)),
            _resident((1, D_FF)),
            _resident((D_FF, D_MODEL)),
            _resident((1, D_MODEL)),
        ],
        out_specs=pl.BlockSpec(memory_space=pl.ANY),
        scratch_shapes=_TILE_SCRATCH + [
            pltpu.VMEM((TM_ROWS, D_MODEL), BF16),
            pltpu.VMEM((CONV_HALO + TM_ROWS, D_FF), F32),
        ],
        compiler_params=_params(1),
        name="conv_ffn_last" if last else "conv_ffn",
    )(xt, mod, w_up, conv_w.reshape(CONV_W, D_FF), conv_b.reshape(1, D_FF), w_down,
      final_g.reshape(1, D_MODEL))


def _s5_prep_kernel(lr_ref, li_ref, ldt_ref, br_ref, bi_ref, ar_ref, ai_ref, bbr_ref, bbi_ref):
    dt = jnp.exp(ldt_ref[...])
    lr = lr_ref[...]
    li = li_ref[...]
    mag = jnp.exp(lr * dt)
    ar = mag * jnp.cos(li * dt)
    ai = mag * jnp.sin(li * dt)
    nr = ar - 1.0
    den = lr * lr + li * li
    fr = (nr * lr + ai * li) / den
    fi = (ai * lr - nr * li) / den
    ar_ref[...] = ar
    ai_ref[...] = ai
    br = br_ref[...]
    bi = bi_ref[...]
    bbr_ref[...] = fr * br - fi * bi
    bbi_ref[...] = fr * bi + fi * br


def _s5_discretise(lam_re, lam_im, log_dt, b_re, b_im):
    g, p, n = S5_GROUPS, S5_STATE, S5_GROUP
    vec = jax.ShapeDtypeStruct((g, 1, p), F32)
    mat = jax.ShapeDtypeStruct((g, n, p), F32)
    return pl.pallas_call(
        _s5_prep_kernel,
        out_shape=(vec, vec, mat, mat),
        name="s5_discretise",
    )(lam_re.reshape(g, 1, p), lam_im.reshape(g, 1, p), log_dt.reshape(g, 1, 1),
      b_re.transpose(0, 2, 1), b_im.transpose(0, 2, 1))


def _block_diagonal(blocks):
    nb, ng, r, c = blocks.shape
    eye = jnp.eye(ng, dtype=blocks.dtype)
    out = blocks[:, :, :, None, :] * eye[None, :, None, :, None]
    return out.reshape(nb, ng * r, ng * c)


def _s5_kernel(x_hbm, mod_ref, win_ref, bm_ref, are_ref, aim_ref, cm_ref, d_ref, wglu_ref, o_hbm,
               xbuf, obuf, sem_in, sem_out, state_ref, h_ref, u_ref, ub_ref, bu_ref, s_ref, y_ref):
    slot_io = _fetch_tile(x_hbm, xbuf, sem_in)

    @pl.when(pl.program_id(0) == 0)
    def _():
        state_ref[...] = jnp.zeros_like(state_ref)

    x = xbuf[slot_io]
    shift = mod_ref[:, 0:D_MODEL][None]
    scale = mod_ref[:, D_MODEL:2 * D_MODEL][None]
    res_gate = mod_ref[:, 2 * D_MODEL:3 * D_MODEL][None]
    h_ref[...] = _norm_mod(x, scale, shift).reshape(TM_ROWS, D_MODEL).astype(BF16)
    u_ref[...] = jnp.dot(h_ref[...], win_ref[...], preferred_element_type=F32)
    ub_ref[...] = u_ref[...].astype(BF16)

    ns = S5_BLOCK_STATES

    def driving_term(cb):
        bu_ref[cb % S5_LOOKAHEAD] = jnp.dot(ub_ref[:, cb * LANES:(cb + 1) * LANES], bm_ref[cb],
                                            preferred_element_type=F32)

    def scan(cb):
        slot = cb % 2
        bu_slot = cb % S5_LOOKAHEAD
        ar = are_ref[cb]
        ai = aim_ref[cb]
        sr = state_ref[cb, :, 0:ns]
        si = state_ref[cb, :, ns:2 * ns]
        for t in range(TM_STEPS):
            rows = slice(t * BATCH, (t + 1) * BATCH)
            sr, si = (ar * sr - ai * si + bu_ref[bu_slot, rows, 0:ns],
                      ar * si + ai * sr + bu_ref[bu_slot, rows, ns:2 * ns])
            s_ref[slot, rows, 0:ns] = sr.astype(BF16)
            s_ref[slot, rows, ns:2 * ns] = si.astype(BF16)
        state_ref[cb, :, 0:ns] = sr
        state_ref[cb, :, ns:2 * ns] = si

    def read_out(cb):
        y_ref[:, cb * LANES:(cb + 1) * LANES] = jnp.dot(
            s_ref[cb % 2], cm_ref[cb], preferred_element_type=F32)

    for cb in range(S5_LOOKAHEAD):
        driving_term(cb)
    for cb in range(S5_BLOCKS + 1):
        if cb < S5_BLOCKS:
            scan(cb)
        if cb >= 1:
            read_out(cb - 1)
        if cb + S5_LOOKAHEAD < S5_BLOCKS:
            driving_term(cb + S5_LOOKAHEAD)

    y = y_ref[...] + d_ref[...] * u_ref[...]
    z = jnp.dot(jax.nn.gelu(y).astype(BF16), wglu_ref[...], preferred_element_type=F32)
    mixed = z[:, :D_MODEL] * _sigmoid(z[:, D_MODEL:])
    obuf[slot_io] = x + res_gate * mixed.reshape(TM_STEPS, BATCH, D_MODEL)
    _emit_tile(o_hbm, obuf, sem_out, slot_io, batch_major=False)


def _s5_mixer(xt, mod, w_in, lam_re, lam_im, log_dt, b_re, b_im, c_re, c_im, d_skip, w_glu):
    ar, ai, bbr, bbi = _s5_discretise(lam_re, lam_im, log_dt, b_re, b_im)
    ng = LANES // S5_GROUP
    nb = S5_BLOCKS
    b_mat = jnp.concatenate([
        _block_diagonal(bbr.reshape(nb, ng, S5_GROUP, S5_STATE)),
        _block_diagonal(bbi.reshape(nb, ng, S5_GROUP, S5_STATE)),
    ], axis=-1).astype(BF16)
    c_mat = jnp.concatenate([
        _block_diagonal(c_re.transpose(0, 2, 1).reshape(nb, ng, S5_STATE, S5_GROUP)),
        _block_diagonal(-c_im.transpose(0, 2, 1).reshape(nb, ng, S5_STATE, S5_GROUP)),
    ], axis=1).astype(BF16)
    a_re = jnp.broadcast_to(ar.reshape(nb, 1, S5_BLOCK_STATES), (nb, BATCH, S5_BLOCK_STATES))
    a_im = jnp.broadcast_to(ai.reshape(nb, 1, S5_BLOCK_STATES), (nb, BATCH, S5_BLOCK_STATES))

    return pl.pallas_call(
        _s5_kernel,
        out_shape=jax.ShapeDtypeStruct((SEQ, BATCH * D_MODEL), F32),
        grid=(SEQ // TM_STEPS,),
        in_specs=[
            pl.BlockSpec(memory_space=pl.ANY),
            _resident((BATCH, 6 * D_MODEL)),
            _resident((D_MODEL, D_MODEL)),
            _resident((nb, LANES, 2 * S5_BLOCK_STATES)),
            _resident((nb, BATCH, S5_BLOCK_STATES)),
            _resident((nb, BATCH, S5_BLOCK_STATES)),
            _resident((nb, 2 * S5_BLOCK_STATES, LANES)),
            _resident((1, D_MODEL)),
            _resident((D_MODEL, 2 * D_MODEL)),
        ],
        out_specs=pl.BlockSpec(memory_space=pl.ANY),
        scratch_shapes=_TILE_SCRATCH + [
            pltpu.VMEM((nb, BATCH, 2 * S5_BLOCK_STATES), F32),
            pltpu.VMEM((TM_ROWS, D_MODEL), BF16),
            pltpu.VMEM((TM_ROWS, D_MODEL), F32),
            pltpu.VMEM((TM_ROWS, D_MODEL), BF16),
            pltpu.VMEM((S5_LOOKAHEAD, TM_ROWS, 2 * S5_BLOCK_STATES), F32),
            pltpu.VMEM((2, TM_ROWS, 2 * S5_BLOCK_STATES), BF16),
            pltpu.VMEM((TM_ROWS, D_MODEL), F32),
        ],
        compiler_params=_params(1),
        name="s5_mixer",
    )(xt, mod, w_in, b_mat, a_re, a_im, c_mat, d_skip.reshape(1, D_MODEL), w_glu)


def kernel(x, c, pos, ada_w, ada_b, ret_w_in, ret_w_out, s5_w_in, s5_lam_re, s5_lam_im, s5_log_dt,
           s5_b_re, s5_b_im, s5_c_re, s5_c_im, s5_d, s5_w_glu, ffn_w_up, ffn_conv_w, ffn_conv_b,
           ffn_w_down, final_norm_g):
    mod = _modulation(c, ada_w, ada_b)
    half = RET_DK // 2
    inv_freq = jnp.power(ROPE_BASE, -jnp.arange(half, dtype=F32) / half).reshape(1, half)

    q, k, v, sg = _ret_projection(x, pos, mod[0], inv_freq, ret_w_in[0].astype(BF16))
    xt = _retention_core(q, k, v, sg, x, mod[0], ret_w_out[0].astype(BF16))
    xt = _conv_ffn(xt, mod[0], ffn_w_up[0].astype(BF16), ffn_conv_w[0], ffn_conv_b[0],
                   ffn_w_down[0].astype(BF16), final_norm_g, last=False)
    xt = _s5_mixer(xt, mod[1], s5_w_in[0].astype(BF16), s5_lam_re[0], s5_lam_im[0], s5_log_dt[0],
                   s5_b_re[0], s5_b_im[0], s5_c_re[0], s5_c_im[0], s5_d[0],
                   s5_w_glu[0].astype(BF16))
    return _conv_ffn(xt, mod[1], ffn_w_up[1].astype(BF16), ffn_conv_w[1], ffn_conv_b[1],
                     ffn_w_down[1].astype(BF16), final_norm_g, last=True)
```

```python
import functools
import math

import jax
import jax.numpy as jnp
from jax import lax
from jax.experimental import pallas as pl
from jax.experimental.pallas import tpu as pltpu

F32 = jnp.float32
BF16 = jnp.bfloat16

D_MODEL = 1024
BATCH = 16
SEQ = 4096
DEPTH = 2
RET_HEADS = 4
RET_DK = 256
RET_DV = 512
RET_QK = RET_HEADS * RET_DK
RET_V = RET_HEADS * RET_DV
RET_PROJ = 2 * RET_QK + 2 * RET_V
RET_CHUNK = 256
ROPE_BASE = 10000.0
S5_GROUP = 16
S5_GROUPS = 64
S5_STATE = 64
D_FF = 2816
CONV_W = 3
EPS = 1e-6

LANES = 128
SUBLANES = 8
VMEM_LIMIT_BYTES = 56 * 1024 * 1024

PROJ_ROWS = 512
RET_ROWS = 512
TM_STEPS = 32
TM_ROWS = TM_STEPS * BATCH
MXU_TILE = 256
FF_CHUNK = 2 * MXU_TILE
FF_CHUNKS = tuple((c, min(c + FF_CHUNK, D_FF)) for c in range(0, D_FF, FF_CHUNK))
CONV_HALO = (CONV_W - 1) * BATCH
S5_BLOCKS = D_MODEL // LANES
S5_BLOCK_STATES = (LANES // S5_GROUP) * S5_STATE
S5_LOOKAHEAD = 3
S5_STEP_BLOCK = 4
MOD_COLS = 1536


def _resident(shape):
    zeros = (0,) * len(shape)
    return pl.BlockSpec(shape, lambda *_: zeros, pipeline_mode=pl.Buffered(1))


def _params(n_axes):
    return pltpu.CompilerParams(
        dimension_semantics=("arbitrary",) * n_axes,
        vmem_limit_bytes=VMEM_LIMIT_BYTES,
    )


def _sigmoid(x):
    return jax.nn.sigmoid(x)


def _tile_in_copy(x_hbm, xbuf, sem, step, slot, b):
    src = x_hbm.at[pl.ds(step * TM_STEPS, TM_STEPS), pl.ds(b * D_MODEL, D_MODEL)]
    return pltpu.make_async_copy(src, xbuf.at[slot, :, b, :], sem.at[slot])


def _tile_out_copy(o_hbm, obuf, sem, step, slot, b, batch_major):
    if batch_major:
        dst = o_hbm.at[b, pl.ds(step * TM_STEPS, TM_STEPS), :]
    else:
        dst = o_hbm.at[pl.ds(step * TM_STEPS, TM_STEPS), pl.ds(b * D_MODEL, D_MODEL)]
    return pltpu.make_async_copy(obuf.at[slot, :, b, :], dst, sem.at[slot])


def _fetch_tile(x_hbm, xbuf, sem):
    i = pl.program_id(0)
    slot = lax.rem(i, 2)

    @pl.when(i == 0)
    def _():
        for b in range(BATCH):
            _tile_in_copy(x_hbm, xbuf, sem, 0, 0, b).start()

    @pl.when(i + 1 < pl.num_programs(0))
    def _():
        for b in range(BATCH):
            _tile_in_copy(x_hbm, xbuf, sem, i + 1, 1 - slot, b).start()

    for b in range(BATCH):
        _tile_in_copy(x_hbm, xbuf, sem, i, slot, b).wait()
    return slot


def _emit_tile(o_hbm, obuf, sem, slot, batch_major):
    i = pl.program_id(0)
    for b in range(BATCH):
        _tile_out_copy(o_hbm, obuf, sem, i, slot, b, batch_major).start()

    @pl.when(i >= 1)
    def _():
        for b in range(BATCH):
            _tile_out_copy(o_hbm, obuf, sem, i - 1, 1 - slot, b, batch_major).wait()

    @pl.when(i == pl.num_programs(0) - 1)
    def _():
        for b in range(BATCH):
            _tile_out_copy(o_hbm, obuf, sem, i, slot, b, batch_major).wait()


_TILE_SCRATCH = [
    pltpu.VMEM((2, TM_STEPS, BATCH, D_MODEL), F32),
    pltpu.VMEM((2, TM_STEPS, BATCH, D_MODEL), F32),
    pltpu.SemaphoreType.DMA((2,)),
    pltpu.SemaphoreType.DMA((2,)),
]


def _norm_mod(x, scale, shift):
    ms = jnp.mean(x * x, axis=-1, keepdims=True)
    return x * lax.rsqrt(ms + EPS) * (1.0 + scale) + shift


def _mod_kernel(c_ref, w_ref, b_ref, o_ref):
    c = c_ref[...]
    cond = c * _sigmoid(c)
    o_ref[0] = jnp.dot(cond, w_ref[0], preferred_element_type=F32,
                       precision=lax.Precision.HIGHEST) + b_ref[0]


def _modulation(c, ada_w, ada_b):
    n_cols = 6 * D_MODEL
    return pl.pallas_call(
        _mod_kernel,
        out_shape=jax.ShapeDtypeStruct((DEPTH, BATCH, n_cols), F32),
        grid=(DEPTH, n_cols // MOD_COLS),
        in_specs=[
            pl.BlockSpec((BATCH, D_MODEL), lambda i, j: (0, 0)),
            pl.BlockSpec((1, D_MODEL, MOD_COLS), lambda i, j: (i, 0, j)),
            pl.BlockSpec((1, 1, MOD_COLS), lambda i, j: (i, 0, j)),
        ],
        out_specs=pl.BlockSpec((1, BATCH, MOD_COLS), lambda i, j: (i, 0, j)),
        compiler_params=_params(2),
        name="adaln_modulation",
    )(c, ada_w, ada_b.reshape(DEPTH, 1, n_cols))


def _rotary_table_part(pos_ref, invf_ref, rot_ref, rows):
    ang = pos_ref[0, rows, :].astype(F32) * invf_ref[...]
    cos = jnp.cos(ang)
    sin = jnp.sin(ang)
    k_scale = RET_DK ** -0.5
    rot_ref[0, rows, :] = cos
    rot_ref[1, rows, :] = sin
    rot_ref[2, rows, :] = cos * k_scale
    rot_ref[3, rows, :] = sin * k_scale
    bits = pltpu.bitcast(cos, jnp.uint32) | pltpu.bitcast(sin, jnp.uint32)
    folded = bits[0:2 * SUBLANES]
    for r0 in range(2 * SUBLANES, bits.shape[0], 2 * SUBLANES):
        folded = folded | bits[r0:r0 + 2 * SUBLANES]
    zeros = lax.shift_right_logical(lax.shift_right_logical(folded, jnp.uint32(16)), jnp.uint32(16))
    return zeros.astype(F32)


def _ret_proj_kernel(x_ref, pos_ref, mod_ref, invf_ref, w_ref,
                     q_ref, k_ref, v_ref, sg_ref, h_ref, rot_ref):
    shift = mod_ref[0, :, 0:D_MODEL]
    scale = mod_ref[0, :, D_MODEL:2 * D_MODEL]
    h_ref[...] = _norm_mod(x_ref[0], scale, shift).astype(BF16)

    part = PROJ_ROWS // (2 * RET_HEADS)
    head = slice(0, 2 * SUBLANES)

    def table_part(j):
        return _rotary_table_part(pos_ref, invf_ref, rot_ref, slice(j * part, (j + 1) * part))

    for hd in range(RET_HEADS):
        c0 = 2 * RET_QK + hd * RET_DV
        cols = slice(hd * RET_DV, (hd + 1) * RET_DV)
        y = jnp.dot(h_ref[...], w_ref[:, c0:c0 + RET_DV], preferred_element_type=F32)
        v_ref[0, :, cols] = y.astype(BF16)
        v_ref[0, head, hd * RET_DV:hd * RET_DV + LANES] = (y[head, 0:LANES] + table_part(hd)).astype(BF16)
    for hd in range(RET_HEADS):
        c0 = 2 * RET_QK + RET_V + hd * RET_DV
        cols = slice(hd * RET_DV, (hd + 1) * RET_DV)
        g = jnp.dot(h_ref[...], w_ref[:, c0:c0 + RET_DV], preferred_element_type=F32)
        sg = g * _sigmoid(g)
        sg_ref[0, :, cols] = sg.astype(BF16)
        sg_ref[0, head, hd * RET_DV:hd * RET_DV + LANES] = (
            sg[head, 0:LANES] + table_part(RET_HEADS + hd)).astype(BF16)

    half = RET_DK // 2
    for hd in range(RET_HEADS):
        for dst, base, table in ((q_ref, 0, 0), (k_ref, RET_QK, 2)):
            c0 = base + hd * RET_DK
            y = jnp.dot(h_ref[...], w_ref[:, c0:c0 + RET_DK], preferred_element_type=F32)
            t1 = y[:, :half]
            t2 = y[:, half:]
            cs = rot_ref[table]
            sn = rot_ref[table + 1]
            o0 = hd * RET_DK
            dst[0, :, o0:o0 + half] = (t1 * cs - t2 * sn).astype(BF16)
            dst[0, :, o0 + half:o0 + RET_DK] = (t1 * sn + t2 * cs).astype(BF16)


def _ret_projection(x, pos, mod0, inv_freq, w_in):
    rows = PROJ_ROWS
    steps = SEQ // rows
    row_map = lambda b, t: (b, t, 0)
    return pl.pallas_call(
        _ret_proj_kernel,
        out_shape=(
            jax.ShapeDtypeStruct((BATCH, SEQ, RET_QK), BF16),
            jax.ShapeDtypeStruct((BATCH, SEQ, RET_QK), BF16),
            jax.ShapeDtypeStruct((BATCH, SEQ, RET_V), BF16),
            jax.ShapeDtypeStruct((BATCH, SEQ, RET_V), BF16),
        ),
        grid=(BATCH, steps),
        in_specs=[
            pl.BlockSpec((1, rows, D_MODEL), row_map),
            pl.BlockSpec((1, rows, 1), row_map),
            pl.BlockSpec((1, 1, 6 * D_MODEL), lambda b, t: (b, 0, 0)),
            _resident((1, RET_DK // 2)),
            _resident((D_MODEL, RET_PROJ)),
        ],
        out_specs=(
            pl.BlockSpec((1, rows, RET_QK), row_map),
            pl.BlockSpec((1, rows, RET_QK), row_map),
            pl.BlockSpec((1, rows, RET_V), row_map),
            pl.BlockSpec((1, rows, RET_V), row_map),
        ),
        scratch_shapes=[
            pltpu.VMEM((rows, D_MODEL), BF16),
            pltpu.VMEM((4, rows, RET_DK // 2), F32),
        ],
        compiler_params=_params(2),
        name="retention_projection",
    )(x, pos.reshape(BATCH, SEQ, 1), mod0.reshape(BATCH, 1, 6 * D_MODEL), inv_freq, w_in)


def _ret_core_kernel(q_ref, k_ref, v_ref, sg_ref, x_ref, mod_ref, intra_ref, cross_ref,
                     tost_ref, decay_ref, wout_ref, o_ref, state_ref, go_ref):
    @pl.when(pl.program_id(1) == 0)
    def _():
        state_ref[...] = jnp.zeros_like(state_ref)

    contract_last = (((1,), (1,)), ((), ()))
    contract_first = (((0,), (0,)), ((), ()))
    for c in range(RET_ROWS // RET_CHUNK):
        rows = slice(c * RET_CHUNK, (c + 1) * RET_CHUNK)
        for hd in range(RET_HEADS):
            qk_cols = slice(hd * RET_DK, (hd + 1) * RET_DK)
            v_cols = slice(hd * RET_DV, (hd + 1) * RET_DV)
            qc = q_ref[0, rows, qk_cols]
            kc = k_ref[0, rows, qk_cols]
            vc = v_ref[0, rows, v_cols]
            state = state_ref[hd]
            s = lax.dot_general(qc, kc, contract_last, preferred_element_type=F32) * intra_ref[hd]
            o = jnp.dot(s.astype(BF16), vc, preferred_element_type=F32)
            o = o + jnp.dot(qc, state.astype(BF16), preferred_element_type=F32) * cross_ref[hd]
            kd = (kc.astype(F32) * tost_ref[hd]).astype(BF16)
            state_ref[hd] = decay_ref[hd] * state + lax.dot_general(
                kd, vc, contract_first, preferred_element_type=F32)
            mu = jnp.mean(o, axis=-1, keepdims=True)
            oc = o - mu
            on = oc * lax.rsqrt(jnp.mean(oc * oc, axis=-1, keepdims=True) + EPS)
            go_ref[rows, v_cols] = (sg_ref[0, rows, v_cols].astype(F32) * on).astype(BF16)

    y = jnp.dot(go_ref[...], wout_ref[...], preferred_element_type=F32)
    gate = mod_ref[0, :, 2 * D_MODEL:3 * D_MODEL]
    o_ref[...] = x_ref[0] + gate * y


def _retention_core(q, k, v, sg, x, mod0, w_out):
    log_gamma = jnp.log1p(-jnp.exp2(-5.0 - jnp.arange(RET_HEADS, dtype=F32)))
    idx = jnp.arange(RET_CHUNK, dtype=F32)
    diff = idx[:, None] - idx[None, :]
    intra = jnp.where(diff[None] >= 0,
                      jnp.exp(log_gamma[:, None, None] * jnp.maximum(diff, 0.0)[None]), 0.0)
    cross = jnp.exp(log_gamma[:, None] * (idx + 1.0))[:, :, None]
    to_state = jnp.exp(log_gamma[:, None] * (RET_CHUNK - 1.0 - idx))[:, :, None]
    chunk_decay = jnp.exp(log_gamma * RET_CHUNK)

    rows = RET_ROWS
    row_map = lambda b, t: (b, t, 0)
    return pl.pallas_call(
        _ret_core_kernel,
        out_shape=jax.ShapeDtypeStruct((SEQ, BATCH * D_MODEL), F32),
        grid=(BATCH, SEQ // rows),
        in_specs=[
            pl.BlockSpec((1, rows, RET_QK), row_map),
            pl.BlockSpec((1, rows, RET_QK), row_map),
            pl.BlockSpec((1, rows, RET_V), row_map),
            pl.BlockSpec((1, rows, RET_V), row_map),
            pl.BlockSpec((1, rows, D_MODEL), row_map),
            pl.BlockSpec((1, 1, 6 * D_MODEL), lambda b, t: (b, 0, 0)),
            _resident((RET_HEADS, RET_CHUNK, RET_CHUNK)),
            _resident((RET_HEADS, RET_CHUNK, 1)),
            _resident((RET_HEADS, RET_CHUNK, 1)),
            pl.BlockSpec(memory_space=pltpu.SMEM),
            _resident((RET_V, D_MODEL)),
        ],
        out_specs=pl.BlockSpec((rows, D_MODEL), lambda b, t: (t, b)),
        scratch_shapes=[
            pltpu.VMEM((RET_HEADS, RET_DK, RET_DV), F32),
            pltpu.VMEM((rows, RET_V), BF16),
        ],
        compiler_params=_params(2),
        name="retention_core",
    )(q, k, v, sg, x, mod0.reshape(BATCH, 1, 6 * D_MODEL), intra, cross, to_state,
      chunk_decay, w_out)


def _ffn_kernel(x_hbm, mod_ref, wup_ref, cw_ref, cb_ref, wdn_ref, fg_ref, o_hbm,
                xbuf, obuf, sem_in, sem_out, h_ref, gate_ref, *, last):
    slot = _fetch_tile(x_hbm, xbuf, sem_in)

    @pl.when(pl.program_id(0) == 0)
    def _():
        gate_ref[0:CONV_HALO, :] = jnp.zeros((CONV_HALO, D_FF), F32)

    x = xbuf[slot]
    shift = mod_ref[:, 3 * D_MODEL:4 * D_MODEL][None]
    scale = mod_ref[:, 4 * D_MODEL:5 * D_MODEL][None]
    res_gate = mod_ref[:, 5 * D_MODEL:6 * D_MODEL][None]
    h_ref[...] = _norm_mod(x, scale, shift).reshape(TM_ROWS, D_MODEL).astype(BF16)

    def up_projection(c0, c1):
        gate_ref[CONV_HALO:CONV_HALO + TM_ROWS, c0:c1] = jnp.dot(
            h_ref[...], wup_ref[:, D_FF + c0:D_FF + c1], preferred_element_type=F32)
        return jnp.dot(h_ref[...], wup_ref[:, c0:c1], preferred_element_type=F32)

    acc = None
    val_next = up_projection(*FF_CHUNKS[0])
    for idx, (c0, c1) in enumerate(FF_CHUNKS):
        val = val_next
        if idx + 1 < len(FF_CHUNKS):
            val_next = up_projection(*FF_CHUNKS[idx + 1])
        conv = cb_ref[:, c0:c1]
        for tap in range(CONV_W):
            conv = conv + cw_ref[tap:tap + 1, c0:c1] * gate_ref[tap * BATCH:tap * BATCH + TM_ROWS, c0:c1]
        act = (conv * _sigmoid(conv) * val).astype(BF16)
        part = jnp.dot(act, wdn_ref[c0:c1, :], preferred_element_type=F32)
        acc = part if acc is None else acc + part

    gate_ref[0:CONV_HALO, :] = gate_ref[TM_ROWS:TM_ROWS + CONV_HALO, :]
    out = x + res_gate * acc.reshape(TM_STEPS, BATCH, D_MODEL)
    if last:
        ms = jnp.mean(out * out, axis=-1, keepdims=True)
        out = out * lax.rsqrt(ms + EPS) * fg_ref[...][None]
    obuf[slot] = out
    _emit_tile(o_hbm, obuf, sem_out, slot, batch_major=last)


def _conv_ffn(xt, mod, w_up, conv_w, conv_b, w_down, final_g, last):
    if last:
        out_shape = jax.ShapeDtypeStruct((BATCH, SEQ, D_MODEL), F32)
    else:
        out_shape = jax.ShapeDtypeStruct((SEQ, BATCH * D_MODEL), F32)
    return pl.pallas_call(
        functools.partial(_ffn_kernel, last=last),
        out_shape=out_shape,
        grid=(SEQ // TM_STEPS,),
        in_specs=[
            pl.BlockSpec(memory_space=pl.ANY),
            _resident((BATCH, 6 * D_MODEL)),
            _resident((D_MODEL, 2 * D_FF)),
            _resident((CONV_W, D_FF)),
            _resident((1, D_FF)),
            _resident((D_FF, D_MODEL)),
            _resident((1, D_MODEL)),
        ],
        out_specs=pl.BlockSpec(memory_space=pl.ANY),
        scratch_shapes=_TILE_SCRATCH + [
            pltpu.VMEM((TM_ROWS, D_MODEL), BF16),
            pltpu.VMEM((CONV_HALO + TM_ROWS, D_FF), F32),
        ],
        compiler_params=_params(1),
        name="conv_ffn_last" if last else "conv_ffn",
    )(xt, mod, w_up, conv_w.reshape(CONV_W, D_FF), conv_b.reshape(1, D_FF), w_down,
      final_g.reshape(1, D_MODEL))


def _cmul(xr, xi, yr, yi):
    return xr * yr - xi * yi, xr * yi + xi * yr


def _s5_prep_kernel(lr_ref, li_ref, ldt_ref, br_ref, bi_ref, cr_ref, ci_ref,
                    amr_ref, ami_ref, wbr_ref, wbi_ref, wcr_ref, wci_ref, f_ref):
    m = S5_STEP_BLOCK
    dt = jnp.exp(ldt_ref[...])
    lr = lr_ref[...]
    li = li_ref[...]
    mag = jnp.exp(lr * dt)
    ar = mag * jnp.cos(li * dt)
    ai = mag * jnp.sin(li * dt)
    nr = ar - 1.0
    den = lr * lr + li * li
    fr = (nr * lr + ai * li) / den
    fi = (ai * lr - nr * li) / den
    bbr, bbi = _cmul(fr, fi, br_ref[...], bi_ref[...])

    pows = [(jnp.ones_like(ar), jnp.zeros_like(ar))]
    for _ in range(m):
        pows.append(_cmul(pows[-1][0], pows[-1][1], ar, ai))
    amr_ref[...], ami_ref[...] = pows[m]

    cr = cr_ref[...]
    ci = ci_ref[...]
    contract_states = (((2,), (2,)), ((0,), (0,)))
    for j in range(m):
        wbr_ref[j], wbi_ref[j] = _cmul(pows[m - 1 - j][0], pows[m - 1 - j][1], bbr, bbi)
        wcr_ref[j], wci_ref[j] = _cmul(pows[j + 1][0], pows[j + 1][1], cr, ci)
        car, cai = _cmul(pows[j][0], pows[j][1], cr, ci)
        f_ref[j] = (lax.dot_general(car, bbr, contract_states, preferred_element_type=F32,
                                    precision=lax.Precision.HIGHEST)
                    - lax.dot_general(cai, bbi, contract_states, preferred_element_type=F32,
                                      precision=lax.Precision.HIGHEST))


def _s5_discretise(lam_re, lam_im, log_dt, b_re, b_im, c_re, c_im):
    g, p, n, m = S5_GROUPS, S5_STATE, S5_GROUP, S5_STEP_BLOCK
    vec = jax.ShapeDtypeStruct((g, 1, p), F32)
    mats = jax.ShapeDtypeStruct((m, g, n, p), F32)
    return pl.pallas_call(
        _s5_prep_kernel,
        out_shape=(vec, vec, mats, mats, mats, mats, jax.ShapeDtypeStruct((m, g, n, n), F32)),
        name="s5_discretise",
    )(lam_re.reshape(g, 1, p), lam_im.reshape(g, 1, p), log_dt.reshape(g, 1, 1),
      b_re.transpose(0, 2, 1), b_im.transpose(0, 2, 1), c_re, c_im)


def _block_diagonal(blocks):
    nb, ng, r, c = blocks.shape
    eye = jnp.eye(ng, dtype=blocks.dtype)
    out = blocks[:, :, :, None, :] * eye[None, :, None, :, None]
    return out.reshape(nb, ng * r, ng * c)


def _s5_kernel(x_hbm, mod_ref, win_ref, wb_ref, amr_ref, ami_ref, wc_ref, wf_ref, d_ref, wglu_ref,
               o_hbm, xbuf, obuf, sem_in, sem_out, state_ref, h_ref, u_ref, ub_ref, bu_ref, s_ref,
               y_ref):
    m = S5_STEP_BLOCK
    nblk = TM_STEPS // m
    rows_blk = nblk * BATCH
    ns = S5_BLOCK_STATES
    slot_io = _fetch_tile(x_hbm, xbuf, sem_in)

    @pl.when(pl.program_id(0) == 0)
    def _():
        state_ref[...] = jnp.zeros_like(state_ref)

    x = xbuf[slot_io]
    shift = mod_ref[:, 0:D_MODEL][None]
    scale = mod_ref[:, D_MODEL:2 * D_MODEL][None]
    res_gate = mod_ref[:, 2 * D_MODEL:3 * D_MODEL][None]
    h_ref[...] = _norm_mod(x, scale, shift).reshape(TM_ROWS, D_MODEL).astype(BF16)
    u_ref[...] = jnp.dot(h_ref[...], win_ref[...], preferred_element_type=F32)
    ub_ref[...] = u_ref[...].astype(BF16).reshape(nblk, m, BATCH, D_MODEL)

    def block_inputs(cb):
        cols = slice(cb * LANES, (cb + 1) * LANES)
        return jnp.concatenate(
            [ub_ref[:, j, :, cols].reshape(rows_blk, LANES) for j in range(m)], axis=-1)

    def driving_term(cb):
        bu_ref[cb % S5_LOOKAHEAD] = jnp.dot(block_inputs(cb), wb_ref[cb],
                                            preferred_element_type=F32)

    def scan(cb):
        slot = cb % 2
        bu_slot = cb % S5_LOOKAHEAD
        ar = amr_ref[cb]
        ai = ami_ref[cb]
        sr = state_ref[cb, :, 0:ns]
        si = state_ref[cb, :, ns:2 * ns]
        for k in range(nblk):
            rows = slice(k * BATCH, (k + 1) * BATCH)
            s_ref[slot, rows, 0:ns] = sr.astype(BF16)
            s_ref[slot, rows, ns:2 * ns] = si.astype(BF16)
            sr, si = (ar * sr - ai * si + bu_ref[bu_slot, rows, 0:ns],
                      ar * si + ai * sr + bu_ref[bu_slot, rows, ns:2 * ns])
        state_ref[cb, :, 0:ns] = sr
        state_ref[cb, :, ns:2 * ns] = si

    def read_out(cb):
        cols = slice(cb * LANES, (cb + 1) * LANES)
        y_blk = (jnp.dot(s_ref[cb % 2], wc_ref[cb], preferred_element_type=F32)
                 + jnp.dot(block_inputs(cb), wf_ref[cb], preferred_element_type=F32))
        for j in range(m):
            y_ref[:, j, :, cols] = y_blk[:, j * LANES:(j + 1) * LANES].reshape(nblk, BATCH, LANES)

    for cb in range(S5_LOOKAHEAD):
        driving_term(cb)
    for cb in range(S5_BLOCKS + 1):
        if cb < S5_BLOCKS:
            scan(cb)
        if cb >= 1:
            read_out(cb - 1)
        if cb + S5_LOOKAHEAD < S5_BLOCKS:
            driving_term(cb + S5_LOOKAHEAD)

    y = y_ref[...].reshape(TM_ROWS, D_MODEL) + d_ref[...] * u_ref[...]
    z = jnp.dot(jax.nn.gelu(y).astype(BF16), wglu_ref[...], preferred_element_type=F32)
    mixed = z[:, :D_MODEL] * _sigmoid(z[:, D_MODEL:])
    obuf[slot_io] = x + res_gate * mixed.reshape(TM_STEPS, BATCH, D_MODEL)
    _emit_tile(o_hbm, obuf, sem_out, slot_io, batch_major=False)


def _s5_mixer(xt, mod, w_in, lam_re, lam_im, log_dt, b_re, b_im, c_re, c_im, d_skip, w_glu):
    amr, ami, wbr, wbi, wcr, wci, feed = _s5_discretise(
        lam_re, lam_im, log_dt, b_re, b_im, c_re, c_im)
    ng = LANES // S5_GROUP
    nb = S5_BLOCKS
    m = S5_STEP_BLOCK
    nblk = TM_STEPS // m
    rows_blk = nblk * BATCH

    def b_block(w):
        return _block_diagonal(w.reshape(nb, ng, S5_GROUP, S5_STATE))

    def c_block(w):
        return _block_diagonal(w.transpose(0, 2, 1).reshape(nb, ng, S5_STATE, S5_GROUP))

    def f_block(w):
        return _block_diagonal(w.transpose(0, 2, 1).reshape(nb, ng, S5_GROUP, S5_GROUP))

    wb = jnp.concatenate(
        [jnp.concatenate([b_block(wbr[j]), b_block(wbi[j])], axis=-1) for j in range(m)],
        axis=1).astype(BF16)
    wc = jnp.concatenate(
        [jnp.concatenate([c_block(wcr[j]), c_block(-wci[j])], axis=1) for j in range(m)],
        axis=-1).astype(BF16)
    zero = jnp.zeros((nb, LANES, LANES), F32)
    wf = jnp.concatenate(
        [jnp.concatenate([f_block(feed[j - i]) if j >= i else zero for j in range(m)], axis=-1)
         for i in range(m)], axis=1).astype(BF16)
    am_re = jnp.broadcast_to(amr.reshape(nb, 1, S5_BLOCK_STATES), (nb, BATCH, S5_BLOCK_STATES))
    am_im = jnp.broadcast_to(ami.reshape(nb, 1, S5_BLOCK_STATES), (nb, BATCH, S5_BLOCK_STATES))

    return pl.pallas_call(
        _s5_kernel,
        out_shape=jax.ShapeDtypeStruct((SEQ, BATCH * D_MODEL), F32),
        grid=(SEQ // TM_STEPS,),
        in_specs=[
            pl.BlockSpec(memory_space=pl.ANY),
            _resident((BATCH, 6 * D_MODEL)),
            _resident((D_MODEL, D_MODEL)),
            _resident((nb, m * LANES, 2 * S5_BLOCK_STATES)),
            _resident((nb, BATCH, S5_BLOCK_STATES)),
            _resident((nb, BATCH, S5_BLOCK_STATES)),
            _resident((nb, 2 * S5_BLOCK_STATES, m * LANES)),
            _resident((nb, m * LANES, m * LANES)),
            _resident((1, D_MODEL)),
            _resident((D_MODEL, 2 * D_MODEL)),
        ],
        out_specs=pl.BlockSpec(memory_space=pl.ANY),
        scratch_shapes=_TILE_SCRATCH + [
            pltpu.VMEM((nb, BATCH, 2 * S5_BLOCK_STATES), F32),
            pltpu.VMEM((TM_ROWS, D_MODEL), BF16),
            pltpu.VMEM((TM_ROWS, D_MODEL), F32),
            pltpu.VMEM((nblk, m, BATCH, D_MODEL), BF16),
            pltpu.VMEM((S5_LOOKAHEAD, rows_blk, 2 * S5_BLOCK_STATES), F32),
            pltpu.VMEM((2, rows_blk, 2 * S5_BLOCK_STATES), BF16),
            pltpu.VMEM((nblk, m, BATCH, D_MODEL), F32),
        ],
        compiler_params=_params(1),
        name="s5_mixer",
    )(xt, mod, w_in, wb, am_re, am_im, wc, wf, d_skip.reshape(1, D_MODEL), w_glu)


def kernel(x, c, pos, ada_w, ada_b, ret_w_in, ret_w_out, s5_w_in, s5_lam_re, s5_lam_im, s5_log_dt,
           s5_b_re, s5_b_im, s5_c_re, s5_c_im, s5_d, s5_w_glu, ffn_w_up, ffn_conv_w, ffn_conv_b,
           ffn_w_down, final_norm_g):
    mod = _modulation(c, ada_w, ada_b)
    half = RET_DK // 2
    inv_freq = jnp.power(ROPE_BASE, -jnp.arange(half, dtype=F32) / half).reshape(1, half)

    q, k, v, sg = _ret_projection(x, pos, mod[0], inv_freq, ret_w_in[0].astype(BF16))
    xt = _retention_core(q, k, v, sg, x, mod[0], ret_w_out[0].astype(BF16))
    xt = _conv_ffn(xt, mod[0], ffn_w_up[0].astype(BF16), ffn_conv_w[0], ffn_conv_b[0],
                   ffn_w_down[0].astype(BF16), final_norm_g, last=False)
    xt = _s5_mixer(xt, mod[1], s5_w_in[0].astype(BF16), s5_lam_re[0], s5_lam_im[0], s5_log_dt[0],
                   s5_b_re[0], s5_b_im[0], s5_c_re[0], s5_c_im[0], s5_d[0],
                   s5_w_glu[0].astype(BF16))
    return _conv_ffn(xt, mod[1], ffn_w_up[1].astype(BF16), ffn_conv_w[1], ffn_conv_b[1],
                     ffn_w_down[1].astype(BF16), final_norm_g, last=True)
```

```python
import functools

import jax
import jax.numpy as jnp
import numpy as np
from jax import lax
from jax.experimental import pallas as pl
from jax.experimental.pallas import tpu as pltpu

F32 = jnp.float32
BF16 = jnp.bfloat16

D_MODEL = 1024
BATCH = 16
SEQ = 4096
DEPTH = 2
RET_HEADS = 4
RET_DK = 256
RET_DV = 512
RET_QK = RET_HEADS * RET_DK
RET_V = RET_HEADS * RET_DV
RET_PROJ = 2 * RET_QK + 2 * RET_V
RET_CHUNK = 256
ROPE_BASE = 10000.0
S5_GROUP = 16
S5_GROUPS = 64
S5_STATE = 64
D_FF = 2816
CONV_W = 3
EPS = 1e-6

LANES = 128
SUBLANES = 8
VMEM_LIMIT_BYTES = 56 * 1024 * 1024

PROJ_ROWS = 512
RET_ROWS = 512
TM_STEPS = 32
TM_ROWS = TM_STEPS * BATCH
MXU_TILE = 256
FF_CHUNK = 2 * MXU_TILE
FF_CHUNKS = tuple((c, min(c + FF_CHUNK, D_FF)) for c in range(0, D_FF, FF_CHUNK))
CONV_HALO = (CONV_W - 1) * BATCH
S5_BLOCKS = D_MODEL // LANES
S5_BLOCK_STATES = (LANES // S5_GROUP) * S5_STATE
S5_LOOKAHEAD = 3
S5_STEP_BLOCK = 4
MOD_COLS = 1536


def _resident(shape):
    zeros = (0,) * len(shape)
    return pl.BlockSpec(shape, lambda *_: zeros, pipeline_mode=pl.Buffered(1))


def _resident_layer(shape, layer):
    zeros = (0,) * len(shape)
    return pl.BlockSpec((None,) + tuple(shape), lambda *_: (layer,) + zeros,
                        pipeline_mode=pl.Buffered(1))


def _params(n_axes):
    return pltpu.CompilerParams(
        dimension_semantics=("arbitrary",) * n_axes,
        vmem_limit_bytes=VMEM_LIMIT_BYTES,
    )


def _sigmoid(x):
    return jax.nn.sigmoid(x)


def _tile_in_copy(x_hbm, xbuf, sem, step, slot, b):
    src = x_hbm.at[pl.ds(step * TM_STEPS, TM_STEPS), pl.ds(b * D_MODEL, D_MODEL)]
    return pltpu.make_async_copy(src, xbuf.at[slot, :, b, :], sem.at[slot])


def _tile_out_copy(o_hbm, obuf, sem, step, slot, b, batch_major):
    if batch_major:
        dst = o_hbm.at[b, pl.ds(step * TM_STEPS, TM_STEPS), :]
    else:
        dst = o_hbm.at[pl.ds(step * TM_STEPS, TM_STEPS), pl.ds(b * D_MODEL, D_MODEL)]
    return pltpu.make_async_copy(obuf.at[slot, :, b, :], dst, sem.at[slot])


def _fetch_tile(x_hbm, xbuf, sem):
    i = pl.program_id(0)
    slot = lax.rem(i, 2)

    @pl.when(i == 0)
    def _():
        for b in range(BATCH):
            _tile_in_copy(x_hbm, xbuf, sem, 0, 0, b).start()

    @pl.when(i + 1 < pl.num_programs(0))
    def _():
        for b in range(BATCH):
            _tile_in_copy(x_hbm, xbuf, sem, i + 1, 1 - slot, b).start()

    for b in range(BATCH):
        _tile_in_copy(x_hbm, xbuf, sem, i, slot, b).wait()
    return slot


def _emit_tile(o_hbm, obuf, sem, slot, batch_major):
    i = pl.program_id(0)
    for b in range(BATCH):
        _tile_out_copy(o_hbm, obuf, sem, i, slot, b, batch_major).start()

    @pl.when(i >= 1)
    def _():
        for b in range(BATCH):
            _tile_out_copy(o_hbm, obuf, sem, i - 1, 1 - slot, b, batch_major).wait()

    @pl.when(i == pl.num_programs(0) - 1)
    def _():
        for b in range(BATCH):
            _tile_out_copy(o_hbm, obuf, sem, i, slot, b, batch_major).wait()


_TILE_SCRATCH = [
    pltpu.VMEM((2, TM_STEPS, BATCH, D_MODEL), F32),
    pltpu.VMEM((2, TM_STEPS, BATCH, D_MODEL), F32),
    pltpu.SemaphoreType.DMA((2,)),
    pltpu.SemaphoreType.DMA((2,)),
]


def _norm_mod(x, scale, shift):
    ms = jnp.mean(x * x, axis=-1, keepdims=True)
    return x * lax.rsqrt(ms + EPS) * (1.0 + scale) + shift


def _mod_kernel(c_ref, w_ref, b_ref, o_ref):
    c = c_ref[...]
    cond = c * _sigmoid(c)
    o_ref[0] = jnp.dot(cond, w_ref[0], preferred_element_type=F32,
                       precision=lax.Precision.HIGHEST) + b_ref[0]


def _modulation(c, ada_w, ada_b):
    n_cols = 6 * D_MODEL
    return pl.pallas_call(
        _mod_kernel,
        out_shape=jax.ShapeDtypeStruct((DEPTH, BATCH, n_cols), F32),
        grid=(DEPTH, n_cols // MOD_COLS),
        in_specs=[
            pl.BlockSpec((BATCH, D_MODEL), lambda i, j: (0, 0)),
            pl.BlockSpec((1, D_MODEL, MOD_COLS), lambda i, j: (i, 0, j)),
            pl.BlockSpec((1, 1, MOD_COLS), lambda i, j: (i, 0, j)),
        ],
        out_specs=pl.BlockSpec((1, BATCH, MOD_COLS), lambda i, j: (i, 0, j)),
        compiler_params=_params(2),
        name="adaln_modulation",
    )(c, ada_w, ada_b.reshape(DEPTH, 1, n_cols))


def _rotary_table_part(pos_ref, invf_ref, rot_ref, rows):
    ang = pos_ref[0, rows, :].astype(F32) * invf_ref[...]
    cos = jnp.cos(ang)
    sin = jnp.sin(ang)
    k_scale = RET_DK ** -0.5
    rot_ref[0, rows, :] = cos
    rot_ref[1, rows, :] = sin
    rot_ref[2, rows, :] = cos * k_scale
    rot_ref[3, rows, :] = sin * k_scale
    bits = pltpu.bitcast(cos, jnp.uint32) | pltpu.bitcast(sin, jnp.uint32)
    folded = bits[0:2 * SUBLANES]
    for r0 in range(2 * SUBLANES, bits.shape[0], 2 * SUBLANES):
        folded = folded | bits[r0:r0 + 2 * SUBLANES]
    zeros = lax.shift_right_logical(lax.shift_right_logical(folded, jnp.uint32(16)), jnp.uint32(16))
    return zeros.astype(F32)


def _ret_proj_kernel(x_ref, pos_ref, mod_ref, invf_ref, w_ref,
                     q_ref, k_ref, v_ref, sg_ref, h_ref, rot_ref):
    shift = mod_ref[0, :, 0:D_MODEL]
    scale = mod_ref[0, :, D_MODEL:2 * D_MODEL]
    h_ref[...] = _norm_mod(x_ref[0], scale, shift).astype(BF16)

    part = PROJ_ROWS // (2 * RET_HEADS)
    head = slice(0, 2 * SUBLANES)

    def table_part(j):
        return _rotary_table_part(pos_ref, invf_ref, rot_ref, slice(j * part, (j + 1) * part))

    for hd in range(RET_HEADS):
        c0 = 2 * RET_QK + hd * RET_DV
        cols = slice(hd * RET_DV, (hd + 1) * RET_DV)
        y = jnp.dot(h_ref[...], w_ref[:, c0:c0 + RET_DV], preferred_element_type=F32)
        v_ref[0, :, cols] = y.astype(BF16)
        v_ref[0, head, hd * RET_DV:hd * RET_DV + LANES] = (y[head, 0:LANES] + table_part(hd)).astype(BF16)
    for hd in range(RET_HEADS):
        c0 = 2 * RET_QK + RET_V + hd * RET_DV
        cols = slice(hd * RET_DV, (hd + 1) * RET_DV)
        g = jnp.dot(h_ref[...], w_ref[:, c0:c0 + RET_DV], preferred_element_type=F32)
        sg = g * _sigmoid(g)
        sg_ref[0, :, cols] = sg.astype(BF16)
        sg_ref[0, head, hd * RET_DV:hd * RET_DV + LANES] = (
            sg[head, 0:LANES] + table_part(RET_HEADS + hd)).astype(BF16)

    half = RET_DK // 2
    for hd in range(RET_HEADS):
        for dst, base, table in ((q_ref, 0, 0), (k_ref, RET_QK, 2)):
            c0 = base + hd * RET_DK
            y = jnp.dot(h_ref[...], w_ref[:, c0:c0 + RET_DK], preferred_element_type=F32)
            t1 = y[:, :half]
            t2 = y[:, half:]
            cs = rot_ref[table]
            sn = rot_ref[table + 1]
            o0 = hd * RET_DK
            dst[0, :, o0:o0 + half] = (t1 * cs - t2 * sn).astype(BF16)
            dst[0, :, o0 + half:o0 + RET_DK] = (t1 * sn + t2 * cs).astype(BF16)


def _ret_projection(x, pos, mod0, inv_freq, w_in):
    rows = PROJ_ROWS
    steps = SEQ // rows
    row_map = lambda b, t: (b, t, 0)
    return pl.pallas_call(
        _ret_proj_kernel,
        out_shape=(
            jax.ShapeDtypeStruct((BATCH, SEQ, RET_QK), BF16),
            jax.ShapeDtypeStruct((BATCH, SEQ, RET_QK), BF16),
            jax.ShapeDtypeStruct((BATCH, SEQ, RET_V), BF16),
            jax.ShapeDtypeStruct((BATCH, SEQ, RET_V), BF16),
        ),
        grid=(BATCH, steps),
        in_specs=[
            pl.BlockSpec((1, rows, D_MODEL), row_map),
            pl.BlockSpec((1, rows, 1), row_map),
            pl.BlockSpec((1, 1, 6 * D_MODEL), lambda b, t: (b, 0, 0)),
            _resident((1, RET_DK // 2)),
            _resident((D_MODEL, RET_PROJ)),
        ],
        out_specs=(
            pl.BlockSpec((1, rows, RET_QK), row_map),
            pl.BlockSpec((1, rows, RET_QK), row_map),
            pl.BlockSpec((1, rows, RET_V), row_map),
            pl.BlockSpec((1, rows, RET_V), row_map),
        ),
        scratch_shapes=[
            pltpu.VMEM((rows, D_MODEL), BF16),
            pltpu.VMEM((4, rows, RET_DK // 2), F32),
        ],
        compiler_params=_params(2),
        name="retention_projection",
    )(x, pos.reshape(BATCH, SEQ, 1), mod0.reshape(BATCH, 1, 6 * D_MODEL), inv_freq, w_in)


def _ret_core_kernel(q_ref, k_ref, v_ref, sg_ref, x_ref, mod_ref, intra_ref, cross_ref,
                     tost_ref, decay_ref, wout_ref, o_ref, state_ref, go_ref):
    @pl.when(pl.program_id(1) == 0)
    def _():
        state_ref[...] = jnp.zeros_like(state_ref)

    contract_last = (((1,), (1,)), ((), ()))
    contract_first = (((0,), (0,)), ((), ()))
    for c in range(RET_ROWS // RET_CHUNK):
        rows = slice(c * RET_CHUNK, (c + 1) * RET_CHUNK)
        for hd in range(RET_HEADS):
            qk_cols = slice(hd * RET_DK, (hd + 1) * RET_DK)
            v_cols = slice(hd * RET_DV, (hd + 1) * RET_DV)
            qc = q_ref[0, rows, qk_cols]
            kc = k_ref[0, rows, qk_cols]
            vc = v_ref[0, rows, v_cols]
            state = state_ref[hd]
            s = lax.dot_general(qc, kc, contract_last, preferred_element_type=F32) * intra_ref[hd]
            o = jnp.dot(s.astype(BF16), vc, preferred_element_type=F32)
            o = o + jnp.dot(qc, state.astype(BF16), preferred_element_type=F32) * cross_ref[hd]
            kd = (kc.astype(F32) * tost_ref[hd]).astype(BF16)
            state_ref[hd] = decay_ref[hd] * state + lax.dot_general(
                kd, vc, contract_first, preferred_element_type=F32)
            mu = jnp.mean(o, axis=-1, keepdims=True)
            oc = o - mu
            on = oc * lax.rsqrt(jnp.mean(oc * oc, axis=-1, keepdims=True) + EPS)
            go_ref[rows, v_cols] = (sg_ref[0, rows, v_cols].astype(F32) * on).astype(BF16)

    y = jnp.dot(go_ref[...], wout_ref[...], preferred_element_type=F32)
    gate = mod_ref[0, :, 2 * D_MODEL:3 * D_MODEL]
    o_ref[...] = x_ref[0] + gate * y


def _retention_core(q, k, v, sg, x, mod0, w_out):
    log_gamma = np.log1p(-np.exp2(-5.0 - np.arange(RET_HEADS, dtype=np.float64)))
    idx = np.arange(RET_CHUNK, dtype=np.float64)
    diff = idx[:, None] - idx[None, :]
    intra = np.where(diff[None] >= 0,
                     np.exp(log_gamma[:, None, None] * np.maximum(diff, 0.0)[None]), 0.0)
    cross = np.exp(log_gamma[:, None] * (idx + 1.0))[:, :, None]
    to_state = np.exp(log_gamma[:, None] * (RET_CHUNK - 1.0 - idx))[:, :, None]
    chunk_decay = np.exp(log_gamma * RET_CHUNK)
    intra, cross, to_state, chunk_decay = (
        jnp.asarray(t, dtype=F32) for t in (intra, cross, to_state, chunk_decay))

    rows = RET_ROWS
    row_map = lambda b, t: (b, t, 0)
    return pl.pallas_call(
        _ret_core_kernel,
        out_shape=jax.ShapeDtypeStruct((SEQ, BATCH * D_MODEL), F32),
        grid=(BATCH, SEQ // rows),
        in_specs=[
            pl.BlockSpec((1, rows, RET_QK), row_map),
            pl.BlockSpec((1, rows, RET_QK), row_map),
            pl.BlockSpec((1, rows, RET_V), row_map),
            pl.BlockSpec((1, rows, RET_V), row_map),
            pl.BlockSpec((1, rows, D_MODEL), row_map),
            pl.BlockSpec((1, 1, 6 * D_MODEL), lambda b, t: (b, 0, 0)),
            _resident((RET_HEADS, RET_CHUNK, RET_CHUNK)),
            _resident((RET_HEADS, RET_CHUNK, 1)),
            _resident((RET_HEADS, RET_CHUNK, 1)),
            pl.BlockSpec(memory_space=pltpu.SMEM),
            _resident((RET_V, D_MODEL)),
        ],
        out_specs=pl.BlockSpec((rows, D_MODEL), lambda b, t: (t, b)),
        scratch_shapes=[
            pltpu.VMEM((RET_HEADS, RET_DK, RET_DV), F32),
            pltpu.VMEM((rows, RET_V), BF16),
        ],
        compiler_params=_params(2),
        name="retention_core",
    )(q, k, v, sg, x, mod0.reshape(BATCH, 1, 6 * D_MODEL), intra, cross, to_state,
      chunk_decay, w_out)


def _ffn_kernel(x_hbm, mod_ref, wup_ref, cw_ref, cb_ref, wdn_ref, fg_ref, o_hbm,
                xbuf, obuf, sem_in, sem_out, h_ref, gate_ref, *, last):
    slot = _fetch_tile(x_hbm, xbuf, sem_in)

    @pl.when(pl.program_id(0) == 0)
    def _():
        gate_ref[0:CONV_HALO, :] = jnp.zeros((CONV_HALO, D_FF), F32)

    x = xbuf[slot]
    shift = mod_ref[:, 3 * D_MODEL:4 * D_MODEL][None]
    scale = mod_ref[:, 4 * D_MODEL:5 * D_MODEL][None]
    res_gate = mod_ref[:, 5 * D_MODEL:6 * D_MODEL][None]
    h_ref[...] = _norm_mod(x, scale, shift).reshape(TM_ROWS, D_MODEL).astype(BF16)

    def up_projection(c0, c1):
        gate_ref[CONV_HALO:CONV_HALO + TM_ROWS, c0:c1] = jnp.dot(
            h_ref[...], wup_ref[:, D_FF + c0:D_FF + c1], preferred_element_type=F32)
        return jnp.dot(h_ref[...], wup_ref[:, c0:c1], preferred_element_type=F32)

    acc = None
    val_next = up_projection(*FF_CHUNKS[0])
    for idx, (c0, c1) in enumerate(FF_CHUNKS):
        val = val_next
        if idx + 1 < len(FF_CHUNKS):
            val_next = up_projection(*FF_CHUNKS[idx + 1])
        conv = cb_ref[:, c0:c1]
        for tap in range(CONV_W):
            conv = conv + cw_ref[tap:tap + 1, c0:c1] * gate_ref[tap * BATCH:tap * BATCH + TM_ROWS, c0:c1]
        act = (conv * _sigmoid(conv) * val).astype(BF16)
        part = jnp.dot(act, wdn_ref[c0:c1, :], preferred_element_type=F32)
        acc = part if acc is None else acc + part

    gate_ref[0:CONV_HALO, :] = gate_ref[TM_ROWS:TM_ROWS + CONV_HALO, :]
    out = x + res_gate * acc.reshape(TM_STEPS, BATCH, D_MODEL)
    if last:
        ms = jnp.mean(out * out, axis=-1, keepdims=True)
        out = out * lax.rsqrt(ms + EPS) * fg_ref[...][None]
    obuf[slot] = out
    _emit_tile(o_hbm, obuf, sem_out, slot, batch_major=last)


def _conv_ffn(xt, layer, mod, w_up, conv_w, conv_b, w_down, final_g):
    last = layer == DEPTH - 1
    if last:
        out_shape = jax.ShapeDtypeStruct((BATCH, SEQ, D_MODEL), F32)
    else:
        out_shape = jax.ShapeDtypeStruct((SEQ, BATCH * D_MODEL), F32)
    return pl.pallas_call(
        functools.partial(_ffn_kernel, last=last),
        out_shape=out_shape,
        grid=(SEQ // TM_STEPS,),
        in_specs=[
            pl.BlockSpec(memory_space=pl.ANY),
            _resident_layer((BATCH, 6 * D_MODEL), layer),
            _resident_layer((D_MODEL, 2 * D_FF), layer),
            _resident_layer((CONV_W, D_FF), layer),
            _resident_layer((1, D_FF), layer),
            _resident_layer((D_FF, D_MODEL), layer),
            _resident((1, D_MODEL)),
        ],
        out_specs=pl.BlockSpec(memory_space=pl.ANY),
        scratch_shapes=_TILE_SCRATCH + [
            pltpu.VMEM((TM_ROWS, D_MODEL), BF16),
            pltpu.VMEM((CONV_HALO + TM_ROWS, D_FF), F32),
        ],
        compiler_params=_params(1),
        name="conv_ffn_last" if last else "conv_ffn",
    )(xt, mod, w_up, conv_w.reshape(DEPTH, CONV_W, D_FF), conv_b.reshape(DEPTH, 1, D_FF), w_down,
      final_g.reshape(1, D_MODEL))


def _cmul(xr, xi, yr, yi):
    return xr * yr - xi * yi, xr * yi + xi * yr


def _s5_prep_kernel(lr_ref, li_ref, ldt_ref, br_ref, bi_ref, cr_ref, ci_ref,
                    amr_ref, ami_ref, wbr_ref, wbi_ref, wcr_ref, wci_ref, f_ref):
    m = S5_STEP_BLOCK
    dt = jnp.exp(ldt_ref[...])
    lr = lr_ref[...]
    li = li_ref[...]
    mag = jnp.exp(lr * dt)
    ar = mag * jnp.cos(li * dt)
    ai = mag * jnp.sin(li * dt)
    nr = ar - 1.0
    den = lr * lr + li * li
    fr = (nr * lr + ai * li) / den
    fi = (ai * lr - nr * li) / den
    bbr, bbi = _cmul(fr, fi, br_ref[...], bi_ref[...])

    pows = [(jnp.ones_like(ar), jnp.zeros_like(ar))]
    for _ in range(m):
        pows.append(_cmul(pows[-1][0], pows[-1][1], ar, ai))
    amr_ref[...], ami_ref[...] = pows[m]

    cr = cr_ref[...]
    ci = ci_ref[...]
    contract_states = (((2,), (2,)), ((0,), (0,)))
    for j in range(m):
        wbr_ref[j], wbi_ref[j] = _cmul(pows[m - 1 - j][0], pows[m - 1 - j][1], bbr, bbi)
        wcr_ref[j], wci_ref[j] = _cmul(pows[j + 1][0], pows[j + 1][1], cr, ci)
        car, cai = _cmul(pows[j][0], pows[j][1], cr, ci)
        f_ref[j] = (lax.dot_general(car, bbr, contract_states, preferred_element_type=F32,
                                    precision=lax.Precision.HIGHEST)
                    - lax.dot_general(cai, bbi, contract_states, preferred_element_type=F32,
                                      precision=lax.Precision.HIGHEST))


def _s5_discretise(lam_re, lam_im, log_dt, b_re, b_im, c_re, c_im):
    g, p, n, m = S5_GROUPS, S5_STATE, S5_GROUP, S5_STEP_BLOCK
    vec = jax.ShapeDtypeStruct((g, 1, p), F32)
    mats = jax.ShapeDtypeStruct((m, g, n, p), F32)
    return pl.pallas_call(
        _s5_prep_kernel,
        out_shape=(vec, vec, mats, mats, mats, mats, jax.ShapeDtypeStruct((m, g, n, n), F32)),
        name="s5_discretise",
    )(lam_re.reshape(g, 1, p), lam_im.reshape(g, 1, p), log_dt.reshape(g, 1, 1),
      b_re.transpose(0, 2, 1), b_im.transpose(0, 2, 1), c_re, c_im)


def _s5_kernel(x_hbm, mod_ref, win_ref, wb_ref, amr_ref, ami_ref, wc_ref, wf_ref, d_ref, wglu_ref,
               o_hbm, xbuf, obuf, sem_in, sem_out, state_ref, h_ref, u_ref, ub_ref, bu_ref, s_ref,
               y_ref):
    m = S5_STEP_BLOCK
    nblk = TM_STEPS // m
    rows_blk = nblk * BATCH
    ns = S5_BLOCK_STATES
    slot_io = _fetch_tile(x_hbm, xbuf, sem_in)

    @pl.when(pl.program_id(0) == 0)
    def _():
        state_ref[...] = jnp.zeros_like(state_ref)

    x = xbuf[slot_io]
    shift = mod_ref[:, 0:D_MODEL][None]
    scale = mod_ref[:, D_MODEL:2 * D_MODEL][None]
    res_gate = mod_ref[:, 2 * D_MODEL:3 * D_MODEL][None]
    h_ref[...] = _norm_mod(x, scale, shift).reshape(TM_ROWS, D_MODEL).astype(BF16)
    u_ref[...] = jnp.dot(h_ref[...], win_ref[...], preferred_element_type=F32)
    ub_ref[...] = u_ref[...].astype(BF16).reshape(nblk, m, BATCH, D_MODEL)

    def block_inputs(cb):
        cols = slice(cb * LANES, (cb + 1) * LANES)
        return jnp.concatenate(
            [ub_ref[:, j, :, cols].reshape(rows_blk, LANES) for j in range(m)], axis=-1)

    def driving_term(cb):
        bu_ref[cb % S5_LOOKAHEAD] = jnp.dot(block_inputs(cb), wb_ref[cb],
                                            preferred_element_type=F32)

    def scan(cb):
        slot = cb % 2
        bu_slot = cb % S5_LOOKAHEAD
        ar = amr_ref[cb]
        ai = ami_ref[cb]
        sr = state_ref[cb, :, 0:ns]
        si = state_ref[cb, :, ns:2 * ns]
        for k in range(nblk):
            rows = slice(k * BATCH, (k + 1) * BATCH)
            s_ref[slot, rows, 0:ns] = sr.astype(BF16)
            s_ref[slot, rows, ns:2 * ns] = si.astype(BF16)
            sr, si = (ar * sr - ai * si + bu_ref[bu_slot, rows, 0:ns],
                      ar * si + ai * sr + bu_ref[bu_slot, rows, ns:2 * ns])
        state_ref[cb, :, 0:ns] = sr
        state_ref[cb, :, ns:2 * ns] = si

    def read_out(cb):
        cols = slice(cb * LANES, (cb + 1) * LANES)
        y_blk = (jnp.dot(s_ref[cb % 2], wc_ref[cb], preferred_element_type=F32)
                 + jnp.dot(block_inputs(cb), wf_ref[cb], preferred_element_type=F32))
        for j in range(m):
            y_ref[:, j, :, cols] = y_blk[:, j * LANES:(j + 1) * LANES].reshape(nblk, BATCH, LANES)

    for cb in range(S5_LOOKAHEAD):
        driving_term(cb)
    for cb in range(S5_BLOCKS + 1):
        if cb < S5_BLOCKS:
            scan(cb)
        if cb >= 1:
            read_out(cb - 1)
        if cb + S5_LOOKAHEAD < S5_BLOCKS:
            driving_term(cb + S5_LOOKAHEAD)

    y = y_ref[...].reshape(TM_ROWS, D_MODEL) + d_ref[...] * u_ref[...]
    h_ref[...] = jax.nn.gelu(y).astype(BF16)
    for c0 in range(0, D_MODEL, MXU_TILE):
        cols = slice(c0, c0 + MXU_TILE)
        gcols = slice(D_MODEL + c0, D_MODEL + c0 + MXU_TILE)
        val = jnp.dot(h_ref[...], wglu_ref[:, cols], preferred_element_type=F32)
        gate = jnp.dot(h_ref[...], wglu_ref[:, gcols], preferred_element_type=F32)
        mixed = (val * _sigmoid(gate)).reshape(TM_STEPS, BATCH, MXU_TILE)
        obuf[slot_io, :, :, cols] = xbuf[slot_io, :, :, cols] + res_gate[:, :, cols] * mixed
    _emit_tile(o_hbm, obuf, sem_out, slot_io, batch_major=False)


def _s5_mixer(xt, mod, w_in, lam_re, lam_im, log_dt, b_re, b_im, c_re, c_im, d_skip, w_glu):
    amr, ami, wbr, wbi, wcr, wci, feed = _s5_discretise(
        lam_re, lam_im, log_dt, b_re, b_im, c_re, c_im)
    ng = LANES // S5_GROUP
    nb = S5_BLOCKS
    m = S5_STEP_BLOCK
    nblk = TM_STEPS // m
    rows_blk = nblk * BATCH

    eye = jnp.eye(ng, dtype=F32)
    per_group = (nb, ng, S5_GROUP, S5_STATE)
    w = jnp.stack([wbr, wbi]).reshape((2, m) + per_group).transpose(2, 1, 3, 4, 0, 5)
    wb = w[:, :, :, :, :, None, :] * eye[None, None, :, None, None, :, None]
    wb = wb.reshape(nb, m * LANES, 2 * S5_BLOCK_STATES).astype(BF16)
    w = jnp.stack([wcr, -wci]).reshape((2, m) + per_group).transpose(2, 0, 5, 1, 3, 4)
    wc = w[:, :, None, :, :, :, :] * eye[None, None, :, None, None, :, None]
    wc = wc.reshape(nb, 2 * S5_BLOCK_STATES, m * LANES).astype(BF16)
    lag = np.arange(m)[None, :] - np.arange(m)[:, None]
    w = jnp.where((lag >= 0)[:, :, None, None, None], feed[np.maximum(lag, 0)], 0.0)
    w = w.reshape(m, m, nb, ng, S5_GROUP, S5_GROUP).transpose(2, 0, 3, 5, 1, 4)
    wf = w[:, :, :, :, :, None, :] * eye[None, None, :, None, None, :, None]
    wf = wf.reshape(nb, m * LANES, m * LANES).astype(BF16)
    am_re = jnp.broadcast_to(amr.reshape(nb, 1, S5_BLOCK_STATES), (nb, BATCH, S5_BLOCK_STATES))
    am_im = jnp.broadcast_to(ami.reshape(nb, 1, S5_BLOCK_STATES), (nb, BATCH, S5_BLOCK_STATES))

    return pl.pallas_call(
        _s5_kernel,
        out_shape=jax.ShapeDtypeStruct((SEQ, BATCH * D_MODEL), F32),
        grid=(SEQ // TM_STEPS,),
        in_specs=[
            pl.BlockSpec(memory_space=pl.ANY),
            _resident((BATCH, 6 * D_MODEL)),
            _resident((D_MODEL, D_MODEL)),
            _resident((nb, m * LANES, 2 * S5_BLOCK_STATES)),
            _resident((nb, BATCH, S5_BLOCK_STATES)),
            _resident((nb, BATCH, S5_BLOCK_STATES)),
            _resident((nb, 2 * S5_BLOCK_STATES, m * LANES)),
            _resident((nb, m * LANES, m * LANES)),
            _resident((1, D_MODEL)),
            _resident((D_MODEL, 2 * D_MODEL)),
        ],
        out_specs=pl.BlockSpec(memory_space=pl.ANY),
        scratch_shapes=_TILE_SCRATCH + [
            pltpu.VMEM((nb, BATCH, 2 * S5_BLOCK_STATES), F32),
            pltpu.VMEM((TM_ROWS, D_MODEL), BF16),
            pltpu.VMEM((TM_ROWS, D_MODEL), F32),
            pltpu.VMEM((nblk, m, BATCH, D_MODEL), BF16),
            pltpu.VMEM((S5_LOOKAHEAD, rows_blk, 2 * S5_BLOCK_STATES), F32),
            pltpu.VMEM((2, rows_blk, 2 * S5_BLOCK_STATES), BF16),
            pltpu.VMEM((nblk, m, BATCH, D_MODEL), F32),
        ],
        compiler_params=_params(1),
        name="s5_mixer",
    )(xt, mod, w_in, wb, am_re, am_im, wc, wf, d_skip.reshape(1, D_MODEL), w_glu)


def kernel(x, c, pos, ada_w, ada_b, ret_w_in, ret_w_out, s5_w_in, s5_lam_re, s5_lam_im, s5_log_dt,
           s5_b_re, s5_b_im, s5_c_re, s5_c_im, s5_d, s5_w_glu, ffn_w_up, ffn_conv_w, ffn_conv_b,
           ffn_w_down, final_norm_g):
    mod = _modulation(c, ada_w, ada_b)
    half = RET_DK // 2
    inv_freq = jnp.power(ROPE_BASE, -jnp.arange(half, dtype=F32) / half).reshape(1, half)

    q, k, v, sg = _ret_projection(x, pos, mod[0], inv_freq, ret_w_in[0].astype(BF16))
    xt = _retention_core(q, k, v, sg, x, mod[0], ret_w_out[0].astype(BF16))
    ffn_params = (mod, ffn_w_up.astype(BF16), ffn_conv_w, ffn_conv_b, ffn_w_down.astype(BF16),
                  final_norm_g)
    xt = _conv_ffn(xt, 0, *ffn_params)
    xt = _s5_mixer(xt, mod[1], s5_w_in[0].astype(BF16), s5_lam_re[0], s5_lam_im[0], s5_log_dt[0],
                   s5_b_re[0], s5_b_im[0], s5_c_re[0], s5_c_im[0], s5_d[0],
                   s5_w_glu[0].astype(BF16))
    return _conv_ffn(xt, 1, *ffn_params)
```

```python
import functools

import jax
import jax.numpy as jnp
import numpy as np
from jax import lax
from jax.experimental import pallas as pl
from jax.experimental.pallas import tpu as pltpu

F32 = jnp.float32
BF16 = jnp.bfloat16

D_MODEL = 1024
BATCH = 16
SEQ = 4096
DEPTH = 2
RET_HEADS = 4
RET_DK = 256
RET_DV = 512
RET_QK = RET_HEADS * RET_DK
RET_V = RET_HEADS * RET_DV
RET_PROJ = 2 * RET_QK + 2 * RET_V
RET_CHUNK = 256
ROPE_BASE = 10000.0
S5_GROUP = 16
S5_GROUPS = 64
S5_STATE = 64
D_FF = 2816
CONV_W = 3
EPS = 1e-6

LANES = 128
SUBLANES = 8
VMEM_LIMIT_BYTES = 56 * 1024 * 1024

PROJ_ROWS = 512
RET_ROWS = 512
TM_STEPS = 32
TM_ROWS = TM_STEPS * BATCH
MXU_TILE = 256
FF_CHUNK = 2 * MXU_TILE
FF_CHUNKS = tuple((c, min(c + FF_CHUNK, D_FF)) for c in range(0, D_FF, FF_CHUNK))
CONV_HALO = (CONV_W - 1) * BATCH
S5_BLOCKS = D_MODEL // LANES
S5_BLOCK_STATES = (LANES // S5_GROUP) * S5_STATE
S5_LOOKAHEAD = 3
S5_STEP_BLOCK = 4
MOD_COLS = 1536


def _resident(shape):
    zeros = (0,) * len(shape)
    return pl.BlockSpec(shape, lambda *_: zeros, pipeline_mode=pl.Buffered(1))


def _resident_layer(shape, layer):
    zeros = (0,) * len(shape)
    return pl.BlockSpec((None,) + tuple(shape), lambda *_: (layer,) + zeros,
                        pipeline_mode=pl.Buffered(1))


def _params(n_axes):
    return pltpu.CompilerParams(
        dimension_semantics=("arbitrary",) * n_axes,
        vmem_limit_bytes=VMEM_LIMIT_BYTES,
    )


def _sigmoid(x):
    return jax.nn.sigmoid(x)


def _tile_in_copy(x_hbm, xbuf, sem, step, slot, b):
    src = x_hbm.at[pl.ds(step * TM_STEPS, TM_STEPS), pl.ds(b * D_MODEL, D_MODEL)]
    return pltpu.make_async_copy(src, xbuf.at[slot, :, b, :], sem.at[slot])


def _tile_out_copy(o_hbm, obuf, sem, step, slot, b, batch_major):
    if batch_major:
        dst = o_hbm.at[b, pl.ds(step * TM_STEPS, TM_STEPS), :]
    else:
        dst = o_hbm.at[pl.ds(step * TM_STEPS, TM_STEPS), pl.ds(b * D_MODEL, D_MODEL)]
    return pltpu.make_async_copy(obuf.at[slot, :, b, :], dst, sem.at[slot])


def _fetch_tile(x_hbm, xbuf, sem):
    i = pl.program_id(0)
    slot = lax.rem(i, 2)

    @pl.when(i == 0)
    def _():
        for b in range(BATCH):
            _tile_in_copy(x_hbm, xbuf, sem, 0, 0, b).start()

    @pl.when(i + 1 < pl.num_programs(0))
    def _():
        for b in range(BATCH):
            _tile_in_copy(x_hbm, xbuf, sem, i + 1, 1 - slot, b).start()

    for b in range(BATCH):
        _tile_in_copy(x_hbm, xbuf, sem, i, slot, b).wait()
    return slot


def _emit_tile(o_hbm, obuf, sem, slot, batch_major):
    i = pl.program_id(0)
    for b in range(BATCH):
        _tile_out_copy(o_hbm, obuf, sem, i, slot, b, batch_major).start()

    @pl.when(i >= 1)
    def _():
        for b in range(BATCH):
            _tile_out_copy(o_hbm, obuf, sem, i - 1, 1 - slot, b, batch_major).wait()

    @pl.when(i == pl.num_programs(0) - 1)
    def _():
        for b in range(BATCH):
            _tile_out_copy(o_hbm, obuf, sem, i, slot, b, batch_major).wait()


_TILE_SCRATCH = [
    pltpu.VMEM((2, TM_STEPS, BATCH, D_MODEL), F32),
    pltpu.VMEM((2, TM_STEPS, BATCH, D_MODEL), F32),
    pltpu.SemaphoreType.DMA((2,)),
    pltpu.SemaphoreType.DMA((2,)),
]


def _norm_mod(x, scale, shift):
    ms = jnp.mean(x * x, axis=-1, keepdims=True)
    return x * lax.rsqrt(ms + EPS) * (1.0 + scale) + shift


def _mod_kernel(c_ref, w_ref, b_ref, o_ref):
    c = c_ref[...]
    cond = c * _sigmoid(c)
    o_ref[0] = jnp.dot(cond, w_ref[0], preferred_element_type=F32,
                       precision=lax.Precision.HIGHEST) + b_ref[0]


def _modulation(c, ada_w, ada_b):
    n_cols = 6 * D_MODEL
    return pl.pallas_call(
        _mod_kernel,
        out_shape=jax.ShapeDtypeStruct((DEPTH, BATCH, n_cols), F32),
        grid=(DEPTH, n_cols // MOD_COLS),
        in_specs=[
            pl.BlockSpec((BATCH, D_MODEL), lambda i, j: (0, 0)),
            pl.BlockSpec((1, D_MODEL, MOD_COLS), lambda i, j: (i, 0, j)),
            pl.BlockSpec((1, 1, MOD_COLS), lambda i, j: (i, 0, j)),
        ],
        out_specs=pl.BlockSpec((1, BATCH, MOD_COLS), lambda i, j: (i, 0, j)),
        compiler_params=_params(2),
        name="adaln_modulation",
    )(c, ada_w, ada_b.reshape(DEPTH, 1, n_cols))


def _rotary_table_part(pos_ref, invf_ref, rot_ref, rows):
    ang = pos_ref[0, rows, :].astype(F32) * invf_ref[...]
    cos = jnp.cos(ang)
    sin = jnp.sin(ang)
    k_scale = RET_DK ** -0.5
    rot_ref[0, rows, :] = cos
    rot_ref[1, rows, :] = sin
    rot_ref[2, rows, :] = cos * k_scale
    rot_ref[3, rows, :] = sin * k_scale
    bits = pltpu.bitcast(cos, jnp.uint32) | pltpu.bitcast(sin, jnp.uint32)
    folded = bits[0:2 * SUBLANES]
    for r0 in range(2 * SUBLANES, bits.shape[0], 2 * SUBLANES):
        folded = folded | bits[r0:r0 + 2 * SUBLANES]
    zeros = lax.shift_right_logical(lax.shift_right_logical(folded, jnp.uint32(16)), jnp.uint32(16))
    return zeros.astype(F32)


def _ret_proj_kernel(x_ref, pos_ref, mod_ref, invf_ref, w_ref,
                     q_ref, k_ref, v_ref, sg_ref, h_ref, rot_ref):
    shift = mod_ref[0, :, 0:D_MODEL]
    scale = mod_ref[0, :, D_MODEL:2 * D_MODEL]
    h_ref[...] = _norm_mod(x_ref[0], scale, shift).astype(BF16)

    part = PROJ_ROWS // (2 * RET_HEADS)
    head = slice(0, 2 * SUBLANES)

    def table_part(j):
        return _rotary_table_part(pos_ref, invf_ref, rot_ref, slice(j * part, (j + 1) * part))

    for hd in range(RET_HEADS):
        c0 = 2 * RET_QK + hd * RET_DV
        cols = slice(hd * RET_DV, (hd + 1) * RET_DV)
        y = jnp.dot(h_ref[...], w_ref[:, c0:c0 + RET_DV], preferred_element_type=F32)
        v_ref[0, :, cols] = y.astype(BF16)
        v_ref[0, head, hd * RET_DV:hd * RET_DV + LANES] = (y[head, 0:LANES] + table_part(hd)).astype(BF16)
    for hd in range(RET_HEADS):
        c0 = 2 * RET_QK + RET_V + hd * RET_DV
        cols = slice(hd * RET_DV, (hd + 1) * RET_DV)
        g = jnp.dot(h_ref[...], w_ref[:, c0:c0 + RET_DV], preferred_element_type=F32)
        sg = g * _sigmoid(g)
        sg_ref[0, :, cols] = sg.astype(BF16)
        sg_ref[0, head, hd * RET_DV:hd * RET_DV + LANES] = (
            sg[head, 0:LANES] + table_part(RET_HEADS + hd)).astype(BF16)

    half = RET_DK // 2
    for hd in range(RET_HEADS):
        for dst, base, table in ((q_ref, 0, 0), (k_ref, RET_QK, 2)):
            c0 = base + hd * RET_DK
            y = jnp.dot(h_ref[...], w_ref[:, c0:c0 + RET_DK], preferred_element_type=F32)
            t1 = y[:, :half]
            t2 = y[:, half:]
            cs = rot_ref[table]
            sn = rot_ref[table + 1]
            o0 = hd * RET_DK
            dst[0, :, o0:o0 + half] = (t1 * cs - t2 * sn).astype(BF16)
            dst[0, :, o0 + half:o0 + RET_DK] = (t1 * sn + t2 * cs).astype(BF16)


def _ret_projection(x, pos, mod0, inv_freq, w_in):
    rows = PROJ_ROWS
    steps = SEQ // rows
    row_map = lambda b, t: (b, t, 0)
    return pl.pallas_call(
        _ret_proj_kernel,
        out_shape=(
            jax.ShapeDtypeStruct((BATCH, SEQ, RET_QK), BF16),
            jax.ShapeDtypeStruct((BATCH, SEQ, RET_QK), BF16),
            jax.ShapeDtypeStruct((BATCH, SEQ, RET_V), BF16),
            jax.ShapeDtypeStruct((BATCH, SEQ, RET_V), BF16),
        ),
        grid=(BATCH, steps),
        in_specs=[
            pl.BlockSpec((1, rows, D_MODEL), row_map),
            pl.BlockSpec((1, rows, 1), row_map),
            pl.BlockSpec((1, 1, 6 * D_MODEL), lambda b, t: (b, 0, 0)),
            _resident((1, RET_DK // 2)),
            _resident((D_MODEL, RET_PROJ)),
        ],
        out_specs=(
            pl.BlockSpec((1, rows, RET_QK), row_map),
            pl.BlockSpec((1, rows, RET_QK), row_map),
            pl.BlockSpec((1, rows, RET_V), row_map),
            pl.BlockSpec((1, rows, RET_V), row_map),
        ),
        scratch_shapes=[
            pltpu.VMEM((rows, D_MODEL), BF16),
            pltpu.VMEM((4, rows, RET_DK // 2), F32),
        ],
        compiler_params=_params(2),
        name="retention_projection",
    )(x, pos.reshape(BATCH, SEQ, 1), mod0.reshape(BATCH, 1, 6 * D_MODEL), inv_freq, w_in)


def _ret_core_kernel(q_ref, k_ref, v_ref, sg_ref, x_ref, mod_ref, intra_ref, cross_ref,
                     tost_ref, decay_ref, wout_ref, o_ref, state_ref, go_ref):
    @pl.when(pl.program_id(1) == 0)
    def _():
        state_ref[...] = jnp.zeros_like(state_ref)

    contract_last = (((1,), (1,)), ((), ()))
    contract_first = (((0,), (0,)), ((), ()))
    for c in range(RET_ROWS // RET_CHUNK):
        rows = slice(c * RET_CHUNK, (c + 1) * RET_CHUNK)
        for hd in range(RET_HEADS):
            qk_cols = slice(hd * RET_DK, (hd + 1) * RET_DK)
            v_cols = slice(hd * RET_DV, (hd + 1) * RET_DV)
            qc = q_ref[0, rows, qk_cols]
            kc = k_ref[0, rows, qk_cols]
            vc = v_ref[0, rows, v_cols]
            state = state_ref[hd]
            s = lax.dot_general(qc, kc, contract_last, preferred_element_type=F32) * intra_ref[hd]
            o = jnp.dot(s.astype(BF16), vc, preferred_element_type=F32)
            o = o + jnp.dot(qc, state.astype(BF16), preferred_element_type=F32) * cross_ref[hd]
            kd = (kc.astype(F32) * tost_ref[hd]).astype(BF16)
            state_ref[hd] = decay_ref[hd] * state + lax.dot_general(
                kd, vc, contract_first, preferred_element_type=F32)
            mu = jnp.mean(o, axis=-1, keepdims=True)
            oc = o - mu
            on = oc * lax.rsqrt(jnp.mean(oc * oc, axis=-1, keepdims=True) + EPS)
            go_ref[rows, v_cols] = (sg_ref[0, rows, v_cols].astype(F32) * on).astype(BF16)

    y = jnp.dot(go_ref[...], wout_ref[...], preferred_element_type=F32)
    gate = mod_ref[0, :, 2 * D_MODEL:3 * D_MODEL]
    o_ref[...] = x_ref[0] + gate * y


def _retention_core(q, k, v, sg, x, mod0, w_out):
    log_gamma = np.log1p(-np.exp2(-5.0 - np.arange(RET_HEADS, dtype=np.float64)))
    idx = np.arange(RET_CHUNK, dtype=np.float64)
    diff = idx[:, None] - idx[None, :]
    intra = np.where(diff[None] >= 0,
                     np.exp(log_gamma[:, None, None] * np.maximum(diff, 0.0)[None]), 0.0)
    cross = np.exp(log_gamma[:, None] * (idx + 1.0))[:, :, None]
    to_state = np.exp(log_gamma[:, None] * (RET_CHUNK - 1.0 - idx))[:, :, None]
    chunk_decay = np.exp(log_gamma * RET_CHUNK)
    intra, cross, to_state, chunk_decay = (
        jnp.asarray(t, dtype=F32) for t in (intra, cross, to_state, chunk_decay))

    rows = RET_ROWS
    row_map = lambda b, t: (b, t, 0)
    return pl.pallas_call(
        _ret_core_kernel,
        out_shape=jax.ShapeDtypeStruct((SEQ, BATCH * D_MODEL), F32),
        grid=(BATCH, SEQ // rows),
        in_specs=[
            pl.BlockSpec((1, rows, RET_QK), row_map),
            pl.BlockSpec((1, rows, RET_QK), row_map),
            pl.BlockSpec((1, rows, RET_V), row_map),
            pl.BlockSpec((1, rows, RET_V), row_map),
            pl.BlockSpec((1, rows, D_MODEL), row_map),
            pl.BlockSpec((1, 1, 6 * D_MODEL), lambda b, t: (b, 0, 0)),
            _resident((RET_HEADS, RET_CHUNK, RET_CHUNK)),
            _resident((RET_HEADS, RET_CHUNK, 1)),
            _resident((RET_HEADS, RET_CHUNK, 1)),
            pl.BlockSpec(memory_space=pltpu.SMEM),
            _resident((RET_V, D_MODEL)),
        ],
        out_specs=pl.BlockSpec((rows, D_MODEL), lambda b, t: (t, b)),
        scratch_shapes=[
            pltpu.VMEM((RET_HEADS, RET_DK, RET_DV), F32),
            pltpu.VMEM((rows, RET_V), BF16),
        ],
        compiler_params=_params(2),
        name="retention_core",
    )(q, k, v, sg, x, mod0.reshape(BATCH, 1, 6 * D_MODEL), intra, cross, to_state,
      chunk_decay, w_out)


def _ffn_kernel(x_hbm, mod_ref, wup_ref, cw_ref, cb_ref, wdn_ref, fg_ref, o_hbm,
                xbuf, obuf, sem_in, sem_out, h_ref, gate_ref, *, last):
    slot = _fetch_tile(x_hbm, xbuf, sem_in)

    @pl.when(pl.program_id(0) == 0)
    def _():
        gate_ref[0:CONV_HALO, :] = jnp.zeros((CONV_HALO, D_FF), F32)

    x = xbuf[slot]
    shift = mod_ref[:, 3 * D_MODEL:4 * D_MODEL][None]
    scale = mod_ref[:, 4 * D_MODEL:5 * D_MODEL][None]
    res_gate = mod_ref[:, 5 * D_MODEL:6 * D_MODEL][None]
    h_ref[...] = _norm_mod(x, scale, shift).reshape(TM_ROWS, D_MODEL).astype(BF16)

    def up_projection(c0, c1):
        gate_ref[CONV_HALO:CONV_HALO + TM_ROWS, c0:c1] = jnp.dot(
            h_ref[...], wup_ref[:, D_FF + c0:D_FF + c1], preferred_element_type=F32)
        return jnp.dot(h_ref[...], wup_ref[:, c0:c1], preferred_element_type=F32)

    acc = None
    val_next = up_projection(*FF_CHUNKS[0])
    for idx, (c0, c1) in enumerate(FF_CHUNKS):
        val = val_next
        if idx + 1 < len(FF_CHUNKS):
            val_next = up_projection(*FF_CHUNKS[idx + 1])
        conv = cb_ref[:, c0:c1]
        for tap in range(CONV_W):
            conv = conv + cw_ref[tap:tap + 1, c0:c1] * gate_ref[tap * BATCH:tap * BATCH + TM_ROWS, c0:c1]
        act = (conv * _sigmoid(conv) * val).astype(BF16)
        part = jnp.dot(act, wdn_ref[c0:c1, :], preferred_element_type=F32)
        acc = part if acc is None else acc + part

    gate_ref[0:CONV_HALO, :] = gate_ref[TM_ROWS:TM_ROWS + CONV_HALO, :]
    out = x + res_gate * acc.reshape(TM_STEPS, BATCH, D_MODEL)
    if last:
        ms = jnp.mean(out * out, axis=-1, keepdims=True)
        out = out * lax.rsqrt(ms + EPS) * fg_ref[...][None]
    obuf[slot] = out
    _emit_tile(o_hbm, obuf, sem_out, slot, batch_major=last)


def _conv_ffn(xt, layer, mod, w_up, conv_w, conv_b, w_down, final_g):
    last = layer == DEPTH - 1
    if last:
        out_shape = jax.ShapeDtypeStruct((BATCH, SEQ, D_MODEL), F32)
    else:
        out_shape = jax.ShapeDtypeStruct((SEQ, BATCH * D_MODEL), F32)
    return pl.pallas_call(
        functools.partial(_ffn_kernel, last=last),
        out_shape=out_shape,
        grid=(SEQ // TM_STEPS,),
        in_specs=[
            pl.BlockSpec(memory_space=pl.ANY),
            _resident_layer((BATCH, 6 * D_MODEL), layer),
            _resident_layer((D_MODEL, 2 * D_FF), layer),
            _resident_layer((CONV_W, D_FF), layer),
            _resident_layer((1, D_FF), layer),
            _resident_layer((D_FF, D_MODEL), layer),
            _resident((1, D_MODEL)),
        ],
        out_specs=pl.BlockSpec(memory_space=pl.ANY),
        scratch_shapes=_TILE_SCRATCH + [
            pltpu.VMEM((TM_ROWS, D_MODEL), BF16),
            pltpu.VMEM((CONV_HALO + TM_ROWS, D_FF), F32),
        ],
        compiler_params=_params(1),
        name="conv_ffn_last" if last else "conv_ffn",
    )(xt, mod, w_up, conv_w.reshape(DEPTH, CONV_W, D_FF), conv_b.reshape(DEPTH, 1, D_FF), w_down,
      final_g.reshape(1, D_MODEL))


def _cmul(xr, xi, yr, yi):
    return xr * yr - xi * yi, xr * yi + xi * yr


def _s5_zoh(lr, li, dt):
    mag = jnp.exp(lr * dt)
    ar = mag * jnp.cos(li * dt)
    ai = mag * jnp.sin(li * dt)
    nr = ar - 1.0
    den = lr * lr + li * li
    return ar, ai, (nr * lr + ai * li) / den, (ai * lr - nr * li) / den


def _powers(ar, ai, n):
    pows = [(jnp.ones_like(ar), jnp.zeros_like(ar))]
    for _ in range(n):
        pows.append(_cmul(pows[-1][0], pows[-1][1], ar, ai))
    return pows


def _s5_prep_kernel(lr_ref, li_ref, ldt_ref, lrt_ref, lit_ref, br_ref, bi_ref, cr_ref, ci_ref,
                    crt_ref, cit_ref, tile_p_ref, tile_n_ref,
                    amr_ref, ami_ref, wb_ref, wc_ref, wf_ref):
    m = S5_STEP_BLOCK
    ng = LANES // S5_GROUP
    dt = jnp.exp(ldt_ref[...])
    ar, ai, fr, fi = _s5_zoh(lr_ref[...], li_ref[...], dt)
    bbr, bbi = _cmul(fr, fi, br_ref[...], bi_ref[...])
    pows = _powers(ar, ai, m)
    amr_ref[...], ami_ref[...] = pows[m]
    art, ait, _, _ = _s5_zoh(lrt_ref[...], lit_ref[...], dt)
    pows_t = _powers(art, ait, m)

    def spread(block, tile_ref, rows_per_group, cols_per_group):
        wide = jnp.dot(block.astype(BF16), tile_ref[...], preferred_element_type=F32)
        rows = lax.broadcasted_iota(jnp.int32, wide.shape, 0)
        cols = lax.broadcasted_iota(jnp.int32, wide.shape, 1)
        row_group = lax.shift_right_logical(rows, rows_per_group.bit_length() - 1) & (ng - 1)
        col_group = lax.shift_right_logical(cols, cols_per_group.bit_length() - 1)
        return jnp.where(row_group == col_group, wide, 0.0).astype(BF16)

    n, p = S5_GROUP, S5_STATE
    cr, ci = cr_ref[...], ci_ref[...]
    crt, cit = crt_ref[...], cit_ref[...]
    contract_states = (((2,), (2,)), ((0,), (0,)))
    wf_ref[...] = jnp.zeros_like(wf_ref)
    for j in range(m):
        wr, wi = _cmul(pows[m - 1 - j][0], pows[m - 1 - j][1], bbr, bbi)
        wr = spread(wr.reshape(S5_GROUPS * n, p), tile_p_ref, n, p)
        wi = spread(wi.reshape(S5_GROUPS * n, p), tile_p_ref, n, p)
        vr, vi = _cmul(pows_t[j + 1][0], pows_t[j + 1][1], crt, cit)
        vr = spread(vr.reshape(S5_GROUPS * p, n), tile_n_ref, p, n)
        vi = spread(-vi.reshape(S5_GROUPS * p, n), tile_n_ref, p, n)
        car, cai = _cmul(pows[j][0], pows[j][1], cr, ci)
        feed = (lax.dot_general(bbr, car, contract_states, preferred_element_type=F32,
                                precision=lax.Precision.HIGHEST)
                - lax.dot_general(bbi, cai, contract_states, preferred_element_type=F32,
                                  precision=lax.Precision.HIGHEST))
        feed = spread(feed.reshape(S5_GROUPS * n, n), tile_n_ref, n, n)
        for cb in range(S5_BLOCKS):
            rb = slice(cb * LANES, (cb + 1) * LANES)
            rc = slice(cb * S5_BLOCK_STATES, (cb + 1) * S5_BLOCK_STATES)
            wb_ref[cb, j * LANES:(j + 1) * LANES, 0:S5_BLOCK_STATES] = wr[rb]
            wb_ref[cb, j * LANES:(j + 1) * LANES, S5_BLOCK_STATES:2 * S5_BLOCK_STATES] = wi[rb]
            wc_ref[cb, 0:S5_BLOCK_STATES, j * LANES:(j + 1) * LANES] = vr[rc]
            wc_ref[cb, S5_BLOCK_STATES:2 * S5_BLOCK_STATES, j * LANES:(j + 1) * LANES] = vi[rc]
            for i in range(m - j):
                wf_ref[cb, i * LANES:(i + 1) * LANES, (i + j) * LANES:(i + j + 1) * LANES] = feed[rb]


def _s5_operands(lam_re, lam_im, log_dt, b_re, b_im, c_re, c_im):
    g, p, n, m, nb = S5_GROUPS, S5_STATE, S5_GROUP, S5_STEP_BLOCK, S5_BLOCKS
    ng = LANES // n
    vec = jax.ShapeDtypeStruct((g, 1, p), F32)
    tile_p = jnp.asarray(np.tile(np.eye(p, dtype=np.float32), (1, ng)), dtype=BF16)
    tile_n = jnp.asarray(np.tile(np.eye(n, dtype=np.float32), (1, ng)), dtype=BF16)
    return pl.pallas_call(
        _s5_prep_kernel,
        out_shape=(vec, vec,
                   jax.ShapeDtypeStruct((nb, m * LANES, 2 * S5_BLOCK_STATES), BF16),
                   jax.ShapeDtypeStruct((nb, 2 * S5_BLOCK_STATES, m * LANES), BF16),
                   jax.ShapeDtypeStruct((nb, m * LANES, m * LANES), BF16)),
        compiler_params=pltpu.CompilerParams(vmem_limit_bytes=VMEM_LIMIT_BYTES),
        name="s5_operands",
    )(lam_re.reshape(g, 1, p), lam_im.reshape(g, 1, p), log_dt.reshape(g, 1, 1),
      lam_re.reshape(g, p, 1), lam_im.reshape(g, p, 1),
      b_re.transpose(0, 2, 1), b_im.transpose(0, 2, 1), c_re, c_im,
      c_re.transpose(0, 2, 1), c_im.transpose(0, 2, 1), tile_p, tile_n)


def _s5_kernel(x_hbm, mod_ref, win_ref, wb_ref, amr_ref, ami_ref, wc_ref, wf_ref, d_ref, wglu_ref,
               o_hbm, xbuf, obuf, sem_in, sem_out, state_ref, h_ref, u_ref, ub_ref, bu_ref, s_ref,
               y_ref):
    m = S5_STEP_BLOCK
    nblk = TM_STEPS // m
    rows_blk = nblk * BATCH
    ns = S5_BLOCK_STATES
    slot_io = _fetch_tile(x_hbm, xbuf, sem_in)

    @pl.when(pl.program_id(0) == 0)
    def _():
        state_ref[...] = jnp.zeros_like(state_ref)

    x = xbuf[slot_io]
    shift = mod_ref[:, 0:D_MODEL][None]
    scale = mod_ref[:, D_MODEL:2 * D_MODEL][None]
    res_gate = mod_ref[:, 2 * D_MODEL:3 * D_MODEL][None]
    h_ref[...] = _norm_mod(x, scale, shift).reshape(TM_ROWS, D_MODEL).astype(BF16)
    u_ref[...] = jnp.dot(h_ref[...], win_ref[...], preferred_element_type=F32)
    ub_ref[...] = u_ref[...].astype(BF16).reshape(nblk, m, BATCH, D_MODEL)

    def block_inputs(cb):
        cols = slice(cb * LANES, (cb + 1) * LANES)
        return jnp.concatenate(
            [ub_ref[:, j, :, cols].reshape(rows_blk, LANES) for j in range(m)], axis=-1)

    def driving_term(cb):
        bu_ref[cb % S5_LOOKAHEAD] = jnp.dot(block_inputs(cb), wb_ref[cb],
                                            preferred_element_type=F32)

    def scan(cb):
        slot = cb % 2
        bu_slot = cb % S5_LOOKAHEAD
        ar = amr_ref[cb]
        ai = ami_ref[cb]
        sr = state_ref[cb, :, 0:ns]
        si = state_ref[cb, :, ns:2 * ns]
        for k in range(nblk):
            rows = slice(k * BATCH, (k + 1) * BATCH)
            s_ref[slot, rows, 0:ns] = sr.astype(BF16)
            s_ref[slot, rows, ns:2 * ns] = si.astype(BF16)
            sr, si = (ar * sr - ai * si + bu_ref[bu_slot, rows, 0:ns],
                      ar * si + ai * sr + bu_ref[bu_slot, rows, ns:2 * ns])
        state_ref[cb, :, 0:ns] = sr
        state_ref[cb, :, ns:2 * ns] = si

    def read_out(cb):
        cols = slice(cb * LANES, (cb + 1) * LANES)
        y_blk = (jnp.dot(s_ref[cb % 2], wc_ref[cb], preferred_element_type=F32)
                 + jnp.dot(block_inputs(cb), wf_ref[cb], preferred_element_type=F32))
        for j in range(m):
            y_ref[:, j, :, cols] = y_blk[:, j * LANES:(j + 1) * LANES].reshape(nblk, BATCH, LANES)

    for cb in range(S5_LOOKAHEAD):
        driving_term(cb)
    for cb in range(S5_BLOCKS + 1):
        if cb < S5_BLOCKS:
            scan(cb)
        if cb >= 1:
            read_out(cb - 1)
        if cb + S5_LOOKAHEAD < S5_BLOCKS:
            driving_term(cb + S5_LOOKAHEAD)

    y = y_ref[...].reshape(TM_ROWS, D_MODEL) + d_ref[...] * u_ref[...]
    h_ref[...] = jax.nn.gelu(y).astype(BF16)
    for c0 in range(0, D_MODEL, MXU_TILE):
        cols = slice(c0, c0 + MXU_TILE)
        gcols = slice(D_MODEL + c0, D_MODEL + c0 + MXU_TILE)
        val = jnp.dot(h_ref[...], wglu_ref[:, cols], preferred_element_type=F32)
        gate = jnp.dot(h_ref[...], wglu_ref[:, gcols], preferred_element_type=F32)
        mixed = (val * _sigmoid(gate)).reshape(TM_STEPS, BATCH, MXU_TILE)
        obuf[slot_io, :, :, cols] = xbuf[slot_io, :, :, cols] + res_gate[:, :, cols] * mixed
    _emit_tile(o_hbm, obuf, sem_out, slot_io, batch_major=False)


def _s5_mixer(xt, mod, w_in, lam_re, lam_im, log_dt, b_re, b_im, c_re, c_im, d_skip, w_glu):
    amr, ami, wb, wc, wf = _s5_operands(lam_re, lam_im, log_dt, b_re, b_im, c_re, c_im)
    nb = S5_BLOCKS
    m = S5_STEP_BLOCK
    nblk = TM_STEPS // m
    rows_blk = nblk * BATCH
    am_re = jnp.broadcast_to(amr.reshape(nb, 1, S5_BLOCK_STATES), (nb, BATCH, S5_BLOCK_STATES))
    am_im = jnp.broadcast_to(ami.reshape(nb, 1, S5_BLOCK_STATES), (nb, BATCH, S5_BLOCK_STATES))

    return pl.pallas_call(
        _s5_kernel,
        out_shape=jax.ShapeDtypeStruct((SEQ, BATCH * D_MODEL), F32),
        grid=(SEQ // TM_STEPS,),
        in_specs=[
            pl.BlockSpec(memory_space=pl.ANY),
            _resident((BATCH, 6 * D_MODEL)),
            _resident((D_MODEL, D_MODEL)),
            _resident((nb, m * LANES, 2 * S5_BLOCK_STATES)),
            _resident((nb, BATCH, S5_BLOCK_STATES)),
            _resident((nb, BATCH, S5_BLOCK_STATES)),
            _resident((nb, 2 * S5_BLOCK_STATES, m * LANES)),
            _resident((nb, m * LANES, m * LANES)),
            _resident((1, D_MODEL)),
            _resident((D_MODEL, 2 * D_MODEL)),
        ],
        out_specs=pl.BlockSpec(memory_space=pl.ANY),
        scratch_shapes=_TILE_SCRATCH + [
            pltpu.VMEM((nb, BATCH, 2 * S5_BLOCK_STATES), F32),
            pltpu.VMEM((TM_ROWS, D_MODEL), BF16),
            pltpu.VMEM((TM_ROWS, D_MODEL), F32),
            pltpu.VMEM((nblk, m, BATCH, D_MODEL), BF16),
            pltpu.VMEM((S5_LOOKAHEAD, rows_blk, 2 * S5_BLOCK_STATES), F32),
            pltpu.VMEM((2, rows_blk, 2 * S5_BLOCK_STATES), BF16),
            pltpu.VMEM((nblk, m, BATCH, D_MODEL), F32),
        ],
        compiler_params=_params(1),
        name="s5_mixer",
    )(xt, mod, w_in, wb, am_re, am_im, wc, wf, d_skip.reshape(1, D_MODEL), w_glu)


def kernel(x, c, pos, ada_w, ada_b, ret_w_in, ret_w_out, s5_w_in, s5_lam_re, s5_lam_im, s5_log_dt,
           s5_b_re, s5_b_im, s5_c_re, s5_c_im, s5_d, s5_w_glu, ffn_w_up, ffn_conv_w, ffn_conv_b,
           ffn_w_down, final_norm_g):
    mod = _modulation(c, ada_w, ada_b)
    half = RET_DK // 2
    inv_freq = jnp.power(ROPE_BASE, -jnp.arange(half, dtype=F32) / half).reshape(1, half)

    q, k, v, sg = _ret_projection(x, pos, mod[0], inv_freq, ret_w_in[0].astype(BF16))
    xt = _retention_core(q, k, v, sg, x, mod[0], ret_w_out[0].astype(BF16))
    ffn_params = (mod, ffn_w_up.astype(BF16), ffn_conv_w, ffn_conv_b, ffn_w_down.astype(BF16),
                  final_norm_g)
    xt = _conv_ffn(xt, 0, *ffn_params)
    xt = _s5_mixer(xt, mod[1], s5_w_in[0].astype(BF16), s5_lam_re[0], s5_lam_im[0], s5_log_dt[0],
                   s5_b_re[0], s5_b_im[0], s5_c_re[0], s5_c_im[0], s5_d[0],
                   s5_w_glu[0].astype(BF16))
    return _conv_ffn(xt, 1, *ffn_params)
```

```python
import functools

import jax
import jax.numpy as jnp
import numpy as np
from jax import lax
from jax.experimental import pallas as pl
from jax.experimental.pallas import tpu as pltpu

F32 = jnp.float32
BF16 = jnp.bfloat16

D_MODEL = 1024
BATCH = 16
SEQ = 4096
DEPTH = 2
RET_HEADS = 4
RET_DK = 256
RET_DV = 512
RET_QK = RET_HEADS * RET_DK
RET_V = RET_HEADS * RET_DV
RET_PROJ = 2 * RET_QK + 2 * RET_V
RET_CHUNK = 256
ROPE_BASE = 10000.0
S5_GROUP = 16
S5_GROUPS = 64
S5_STATE = 64
D_FF = 2816
CONV_W = 3
EPS = 1e-6

LANES = 128
SUBLANES = 8
VMEM_LIMIT_BYTES = 56 * 1024 * 1024

PROJ_ROWS = 512
RET_ROWS = 512
TM_STEPS = 32
TM_ROWS = TM_STEPS * BATCH
MXU_TILE = 256
FF_CHUNK = 2 * MXU_TILE
FF_CHUNKS = tuple((c, min(c + FF_CHUNK, D_FF)) for c in range(0, D_FF, FF_CHUNK))
CONV_HALO = (CONV_W - 1) * BATCH
S5_BLOCKS = D_MODEL // LANES
S5_BLOCK_STATES = (LANES // S5_GROUP) * S5_STATE
S5_LOOKAHEAD = 3
S5_STEP_BLOCK = 4
MOD_COLS = 1536


def _resident(shape):
    zeros = (0,) * len(shape)
    return pl.BlockSpec(shape, lambda *_: zeros, pipeline_mode=pl.Buffered(1))


def _resident_layer(shape, layer):
    zeros = (0,) * len(shape)
    return pl.BlockSpec((None,) + tuple(shape), lambda *_: (layer,) + zeros,
                        pipeline_mode=pl.Buffered(1))


def _params(n_axes):
    return pltpu.CompilerParams(
        dimension_semantics=("arbitrary",) * n_axes,
        vmem_limit_bytes=VMEM_LIMIT_BYTES,
    )


def _sigmoid(x):
    return jax.nn.sigmoid(x)


def _tile_in_copy(x_hbm, xbuf, sem, step, slot, b):
    src = x_hbm.at[pl.ds(step * TM_STEPS, TM_STEPS), pl.ds(b * D_MODEL, D_MODEL)]
    return pltpu.make_async_copy(src, xbuf.at[slot, :, b, :], sem.at[slot])


def _tile_out_copy(o_hbm, obuf, sem, step, slot, b, batch_major):
    if batch_major:
        dst = o_hbm.at[b, pl.ds(step * TM_STEPS, TM_STEPS), :]
    else:
        dst = o_hbm.at[pl.ds(step * TM_STEPS, TM_STEPS), pl.ds(b * D_MODEL, D_MODEL)]
    return pltpu.make_async_copy(obuf.at[slot, :, b, :], dst, sem.at[slot])


def _fetch_tile(x_hbm, xbuf, sem):
    i = pl.program_id(0)
    slot = lax.rem(i, 2)

    @pl.when(i == 0)
    def _():
        for b in range(BATCH):
            _tile_in_copy(x_hbm, xbuf, sem, 0, 0, b).start()

    for b in range(BATCH):
        _tile_in_copy(x_hbm, xbuf, sem, i, slot, b).wait()
    return slot


def _prefetch_next_tile(x_hbm, xbuf, sem, slot):
    i = pl.program_id(0)
    nxt = jnp.minimum(i + 1, pl.num_programs(0) - 1)
    for b in range(BATCH):
        _tile_in_copy(x_hbm, xbuf, sem, nxt, 1 - slot, b).start()


def _drain_prefetch(x_hbm, xbuf, sem, slot):
    i = pl.program_id(0)

    @pl.when(i == pl.num_programs(0) - 1)
    def _():
        for b in range(BATCH):
            _tile_in_copy(x_hbm, xbuf, sem, i, 1 - slot, b).wait()


def _emit_tile(o_hbm, obuf, sem, slot, batch_major):
    i = pl.program_id(0)
    for b in range(BATCH):
        _tile_out_copy(o_hbm, obuf, sem, i, slot, b, batch_major).start()

    @pl.when(i >= 1)
    def _():
        for b in range(BATCH):
            _tile_out_copy(o_hbm, obuf, sem, i - 1, 1 - slot, b, batch_major).wait()

    @pl.when(i == pl.num_programs(0) - 1)
    def _():
        for b in range(BATCH):
            _tile_out_copy(o_hbm, obuf, sem, i, slot, b, batch_major).wait()


_TILE_SCRATCH = [
    pltpu.VMEM((2, TM_STEPS, BATCH, D_MODEL), F32),
    pltpu.VMEM((2, TM_STEPS, BATCH, D_MODEL), F32),
    pltpu.SemaphoreType.DMA((2,)),
    pltpu.SemaphoreType.DMA((2,)),
]


def _norm_mod(x, scale, shift):
    ms = jnp.mean(x * x, axis=-1, keepdims=True)
    return x * lax.rsqrt(ms + EPS) * (1.0 + scale) + shift


def _mod_kernel(c_ref, w_ref, b_ref, o_ref):
    c = c_ref[...]
    cond = c * _sigmoid(c)
    o_ref[0] = jnp.dot(cond, w_ref[0], preferred_element_type=F32,
                       precision=lax.Precision.HIGHEST) + b_ref[0]


def _modulation(c, ada_w, ada_b):
    n_cols = 6 * D_MODEL
    return pl.pallas_call(
        _mod_kernel,
        out_shape=jax.ShapeDtypeStruct((DEPTH, BATCH, n_cols), F32),
        grid=(DEPTH, n_cols // MOD_COLS),
        in_specs=[
            pl.BlockSpec((BATCH, D_MODEL), lambda i, j: (0, 0)),
            pl.BlockSpec((1, D_MODEL, MOD_COLS), lambda i, j: (i, 0, j)),
            pl.BlockSpec((1, 1, MOD_COLS), lambda i, j: (i, 0, j)),
        ],
        out_specs=pl.BlockSpec((1, BATCH, MOD_COLS), lambda i, j: (i, 0, j)),
        compiler_params=_params(2),
        name="adaln_modulation",
    )(c, ada_w, ada_b.reshape(DEPTH, 1, n_cols))


def _rotary_table_part(pos_ref, invf_ref, rot_ref, rows):
    ang = pos_ref[0, rows, :].astype(F32) * invf_ref[...]
    cos = jnp.cos(ang)
    sin = jnp.sin(ang)
    k_scale = RET_DK ** -0.5
    rot_ref[0, rows, :] = cos
    rot_ref[1, rows, :] = sin
    rot_ref[2, rows, :] = cos * k_scale
    rot_ref[3, rows, :] = sin * k_scale
    bits = pltpu.bitcast(cos, jnp.uint32) | pltpu.bitcast(sin, jnp.uint32)
    folded = bits[0:2 * SUBLANES]
    for r0 in range(2 * SUBLANES, bits.shape[0], 2 * SUBLANES):
        folded = folded | bits[r0:r0 + 2 * SUBLANES]
    zeros = lax.shift_right_logical(lax.shift_right_logical(folded, jnp.uint32(16)), jnp.uint32(16))
    return zeros.astype(F32)


def _ret_proj_kernel(x_ref, pos_ref, mod_ref, invf_ref, w_ref,
                     q_ref, k_ref, v_ref, sg_ref, h_ref, rot_ref):
    shift = mod_ref[0, :, 0:D_MODEL]
    scale = mod_ref[0, :, D_MODEL:2 * D_MODEL]
    h_ref[...] = _norm_mod(x_ref[0], scale, shift).astype(BF16)

    part = PROJ_ROWS // (2 * RET_HEADS)
    head = slice(0, 2 * SUBLANES)

    def table_part(j):
        return _rotary_table_part(pos_ref, invf_ref, rot_ref, slice(j * part, (j + 1) * part))

    for hd in range(RET_HEADS):
        c0 = 2 * RET_QK + hd * RET_DV
        cols = slice(hd * RET_DV, (hd + 1) * RET_DV)
        y = jnp.dot(h_ref[...], w_ref[:, c0:c0 + RET_DV], preferred_element_type=F32)
        v_ref[0, :, cols] = y.astype(BF16)
        v_ref[0, head, hd * RET_DV:hd * RET_DV + LANES] = (y[head, 0:LANES] + table_part(hd)).astype(BF16)
    for hd in range(RET_HEADS):
        c0 = 2 * RET_QK + RET_V + hd * RET_DV
        cols = slice(hd * RET_DV, (hd + 1) * RET_DV)
        g = jnp.dot(h_ref[...], w_ref[:, c0:c0 + RET_DV], preferred_element_type=F32)
        sg = g * _sigmoid(g)
        sg_ref[0, :, cols] = sg.astype(BF16)
        sg_ref[0, head, hd * RET_DV:hd * RET_DV + LANES] = (
            sg[head, 0:LANES] + table_part(RET_HEADS + hd)).astype(BF16)

    half = RET_DK // 2
    for hd in range(RET_HEADS):
        for dst, base, table in ((q_ref, 0, 0), (k_ref, RET_QK, 2)):
            c0 = base + hd * RET_DK
            y = jnp.dot(h_ref[...], w_ref[:, c0:c0 + RET_DK], preferred_element_type=F32)
            t1 = y[:, :half]
            t2 = y[:, half:]
            cs = rot_ref[table]
            sn = rot_ref[table + 1]
            o0 = hd * RET_DK
            dst[0, :, o0:o0 + half] = (t1 * cs - t2 * sn).astype(BF16)
            dst[0, :, o0 + half:o0 + RET_DK] = (t1 * sn + t2 * cs).astype(BF16)


def _ret_projection(x, pos, mod0, inv_freq, w_in):
    rows = PROJ_ROWS
    steps = SEQ // rows
    row_map = lambda b, t: (b, t, 0)
    return pl.pallas_call(
        _ret_proj_kernel,
        out_shape=(
            jax.ShapeDtypeStruct((BATCH, SEQ, RET_QK), BF16),
            jax.ShapeDtypeStruct((BATCH, SEQ, RET_QK), BF16),
            jax.ShapeDtypeStruct((BATCH, SEQ, RET_V), BF16),
            jax.ShapeDtypeStruct((BATCH, SEQ, RET_V), BF16),
        ),
        grid=(BATCH, steps),
        in_specs=[
            pl.BlockSpec((1, rows, D_MODEL), row_map),
            pl.BlockSpec((1, rows, 1), row_map),
            pl.BlockSpec((1, 1, 6 * D_MODEL), lambda b, t: (b, 0, 0)),
            _resident((1, RET_DK // 2)),
            _resident((D_MODEL, RET_PROJ)),
        ],
        out_specs=(
            pl.BlockSpec((1, rows, RET_QK), row_map),
            pl.BlockSpec((1, rows, RET_QK), row_map),
            pl.BlockSpec((1, rows, RET_V), row_map),
            pl.BlockSpec((1, rows, RET_V), row_map),
        ),
        scratch_shapes=[
            pltpu.VMEM((rows, D_MODEL), BF16),
            pltpu.VMEM((4, rows, RET_DK // 2), F32),
        ],
        compiler_params=_params(2),
        name="retention_projection",
    )(x, pos.reshape(BATCH, SEQ, 1), mod0.reshape(BATCH, 1, 6 * D_MODEL), inv_freq, w_in)


def _ret_core_kernel(q_ref, k_ref, v_ref, sg_ref, x_ref, mod_ref, intra_ref, cross_ref,
                     tost_ref, decay_ref, wout_ref, o_ref, state_ref, go_ref):
    @pl.when(pl.program_id(1) == 0)
    def _():
        state_ref[...] = jnp.zeros_like(state_ref)

    contract_last = (((1,), (1,)), ((), ()))
    contract_first = (((0,), (0,)), ((), ()))
    for c in range(RET_ROWS // RET_CHUNK):
        rows = slice(c * RET_CHUNK, (c + 1) * RET_CHUNK)
        for hd in range(RET_HEADS):
            qk_cols = slice(hd * RET_DK, (hd + 1) * RET_DK)
            v_cols = slice(hd * RET_DV, (hd + 1) * RET_DV)
            qc = q_ref[0, rows, qk_cols]
            kc = k_ref[0, rows, qk_cols]
            vc = v_ref[0, rows, v_cols]
            state = state_ref[hd]
            s = lax.dot_general(qc, kc, contract_last, preferred_element_type=F32) * intra_ref[hd]
            o = jnp.dot(s.astype(BF16), vc, preferred_element_type=F32)
            o = o + jnp.dot(qc, state.astype(BF16), preferred_element_type=F32) * cross_ref[hd]
            kd = (kc.astype(F32) * tost_ref[hd]).astype(BF16)
            state_ref[hd] = decay_ref[hd] * state + lax.dot_general(
                kd, vc, contract_first, preferred_element_type=F32)
            mu = jnp.mean(o, axis=-1, keepdims=True)
            oc = o - mu
            on = oc * lax.rsqrt(jnp.mean(oc * oc, axis=-1, keepdims=True) + EPS)
            go_ref[rows, v_cols] = (sg_ref[0, rows, v_cols].astype(F32) * on).astype(BF16)

    y = jnp.dot(go_ref[...], wout_ref[...], preferred_element_type=F32)
    gate = mod_ref[0, :, 2 * D_MODEL:3 * D_MODEL]
    o_ref[...] = x_ref[0] + gate * y


def _retention_core(q, k, v, sg, x, mod0, w_out):
    log_gamma = np.log1p(-np.exp2(-5.0 - np.arange(RET_HEADS, dtype=np.float64)))
    idx = np.arange(RET_CHUNK, dtype=np.float64)
    diff = idx[:, None] - idx[None, :]
    intra = np.where(diff[None] >= 0,
                     np.exp(log_gamma[:, None, None] * np.maximum(diff, 0.0)[None]), 0.0)
    cross = np.exp(log_gamma[:, None] * (idx + 1.0))[:, :, None]
    to_state = np.exp(log_gamma[:, None] * (RET_CHUNK - 1.0 - idx))[:, :, None]
    chunk_decay = np.exp(log_gamma * RET_CHUNK)
    intra, cross, to_state, chunk_decay = (
        jnp.asarray(t, dtype=F32) for t in (intra, cross, to_state, chunk_decay))

    rows = RET_ROWS
    row_map = lambda b, t: (b, t, 0)
    return pl.pallas_call(
        _ret_core_kernel,
        out_shape=jax.ShapeDtypeStruct((SEQ, BATCH * D_MODEL), F32),
        grid=(BATCH, SEQ // rows),
        in_specs=[
            pl.BlockSpec((1, rows, RET_QK), row_map),
            pl.BlockSpec((1, rows, RET_QK), row_map),
            pl.BlockSpec((1, rows, RET_V), row_map),
            pl.BlockSpec((1, rows, RET_V), row_map),
            pl.BlockSpec((1, rows, D_MODEL), row_map),
            pl.BlockSpec((1, 1, 6 * D_MODEL), lambda b, t: (b, 0, 0)),
            _resident((RET_HEADS, RET_CHUNK, RET_CHUNK)),
            _resident((RET_HEADS, RET_CHUNK, 1)),
            _resident((RET_HEADS, RET_CHUNK, 1)),
            pl.BlockSpec(memory_space=pltpu.SMEM),
            _resident((RET_V, D_MODEL)),
        ],
        out_specs=pl.BlockSpec((rows, D_MODEL), lambda b, t: (t, b)),
        scratch_shapes=[
            pltpu.VMEM((RET_HEADS, RET_DK, RET_DV), F32),
            pltpu.VMEM((rows, RET_V), BF16),
        ],
        compiler_params=_params(2),
        name="retention_core",
    )(q, k, v, sg, x, mod0.reshape(BATCH, 1, 6 * D_MODEL), intra, cross, to_state,
      chunk_decay, w_out)


def _ffn_kernel(x_hbm, mod_ref, wup_ref, cw_ref, cb_ref, wdn_ref, fg_ref, o_hbm,
                xbuf, obuf, sem_in, sem_out, h_ref, gate_ref, *, last):
    slot = _fetch_tile(x_hbm, xbuf, sem_in)

    @pl.when(pl.program_id(0) == 0)
    def _():
        gate_ref[0:CONV_HALO, :] = jnp.zeros((CONV_HALO, D_FF), F32)

    x = xbuf[slot]
    shift = mod_ref[:, 3 * D_MODEL:4 * D_MODEL][None]
    scale = mod_ref[:, 4 * D_MODEL:5 * D_MODEL][None]
    res_gate = mod_ref[:, 5 * D_MODEL:6 * D_MODEL][None]
    h_ref[...] = _norm_mod(x, scale, shift).reshape(TM_ROWS, D_MODEL).astype(BF16)

    def up_projection(c0, c1):
        gate_ref[CONV_HALO:CONV_HALO + TM_ROWS, c0:c1] = jnp.dot(
            h_ref[...], wup_ref[:, D_FF + c0:D_FF + c1], preferred_element_type=F32)
        return jnp.dot(h_ref[...], wup_ref[:, c0:c1], preferred_element_type=F32)

    acc = None
    val_next = up_projection(*FF_CHUNKS[0])
    _prefetch_next_tile(x_hbm, xbuf, sem_in, slot)
    for idx, (c0, c1) in enumerate(FF_CHUNKS):
        val = val_next
        if idx + 1 < len(FF_CHUNKS):
            val_next = up_projection(*FF_CHUNKS[idx + 1])
        conv = cb_ref[:, c0:c1]
        for tap in range(CONV_W):
            conv = conv + cw_ref[tap:tap + 1, c0:c1] * gate_ref[tap * BATCH:tap * BATCH + TM_ROWS, c0:c1]
        act = (conv * _sigmoid(conv) * val).astype(BF16)
        part = jnp.dot(act, wdn_ref[c0:c1, :], preferred_element_type=F32)
        acc = part if acc is None else acc + part

    gate_ref[0:CONV_HALO, :] = gate_ref[TM_ROWS:TM_ROWS + CONV_HALO, :]
    out = x + res_gate * acc.reshape(TM_STEPS, BATCH, D_MODEL)
    if last:
        ms = jnp.mean(out * out, axis=-1, keepdims=True)
        out = out * lax.rsqrt(ms + EPS) * fg_ref[...][None]
    obuf[slot] = out
    _emit_tile(o_hbm, obuf, sem_out, slot, batch_major=last)
    _drain_prefetch(x_hbm, xbuf, sem_in, slot)


def _conv_ffn(xt, layer, mod, w_up, conv_w, conv_b, w_down, final_g):
    last = layer == DEPTH - 1
    if last:
        out_shape = jax.ShapeDtypeStruct((BATCH, SEQ, D_MODEL), F32)
    else:
        out_shape = jax.ShapeDtypeStruct((SEQ, BATCH * D_MODEL), F32)
    return pl.pallas_call(
        functools.partial(_ffn_kernel, last=last),
        out_shape=out_shape,
        grid=(SEQ // TM_STEPS,),
        in_specs=[
            pl.BlockSpec(memory_space=pl.ANY),
            _resident_layer((BATCH, 6 * D_MODEL), layer),
            _resident_layer((D_MODEL, 2 * D_FF), layer),
            _resident_layer((CONV_W, D_FF), layer),
            _resident_layer((1, D_FF), layer),
            _resident_layer((D_FF, D_MODEL), layer),
            _resident((1, D_MODEL)),
        ],
        out_specs=pl.BlockSpec(memory_space=pl.ANY),
        scratch_shapes=_TILE_SCRATCH + [
            pltpu.VMEM((TM_ROWS, D_MODEL), BF16),
            pltpu.VMEM((CONV_HALO + TM_ROWS, D_FF), F32),
        ],
        compiler_params=_params(1),
        name="conv_ffn_last" if last else "conv_ffn",
    )(xt, mod, w_up, conv_w.reshape(DEPTH, CONV_W, D_FF), conv_b.reshape(DEPTH, 1, D_FF), w_down,
      final_g.reshape(1, D_MODEL))


def _cmul(xr, xi, yr, yi):
    return xr * yr - xi * yi, xr * yi + xi * yr


def _s5_zoh(lr, li, dt):
    mag = jnp.exp(lr * dt)
    ar = mag * jnp.cos(li * dt)
    ai = mag * jnp.sin(li * dt)
    nr = ar - 1.0
    den = lr * lr + li * li
    return ar, ai, (nr * lr + ai * li) / den, (ai * lr - nr * li) / den


def _powers(ar, ai, n):
    pows = [(jnp.ones_like(ar), jnp.zeros_like(ar))]
    for _ in range(n):
        pows.append(_cmul(pows[-1][0], pows[-1][1], ar, ai))
    return pows


def _s5_prep_kernel(lr_ref, li_ref, ldt_ref, lrt_ref, lit_ref, br_ref, bi_ref, cr_ref, ci_ref,
                    crt_ref, cit_ref, tile_p_ref, tile_n_ref,
                    amr_ref, ami_ref, wb_ref, wc_ref, wf_ref):
    m = S5_STEP_BLOCK
    ng = LANES // S5_GROUP
    dt = jnp.exp(ldt_ref[...])
    ar, ai, fr, fi = _s5_zoh(lr_ref[...], li_ref[...], dt)
    bbr, bbi = _cmul(fr, fi, br_ref[...], bi_ref[...])
    pows = _powers(ar, ai, m)
    amr_ref[...], ami_ref[...] = pows[m]
    art, ait, _, _ = _s5_zoh(lrt_ref[...], lit_ref[...], dt)
    pows_t = _powers(art, ait, m)

    def spread(block, tile_ref, rows_per_group, cols_per_group):
        wide = jnp.dot(block.astype(BF16), tile_ref[...], preferred_element_type=F32)
        rows = lax.broadcasted_iota(jnp.int32, wide.shape, 0)
        cols = lax.broadcasted_iota(jnp.int32, wide.shape, 1)
        row_group = lax.shift_right_logical(rows, rows_per_group.bit_length() - 1) & (ng - 1)
        col_group = lax.shift_right_logical(cols, cols_per_group.bit_length() - 1)
        return jnp.where(row_group == col_group, wide, 0.0).astype(BF16)

    n, p = S5_GROUP, S5_STATE
    cr, ci = cr_ref[...], ci_ref[...]
    crt, cit = crt_ref[...], cit_ref[...]
    contract_states = (((2,), (2,)), ((0,), (0,)))
    wf_ref[...] = jnp.zeros_like(wf_ref)
    for j in range(m):
        wr, wi = _cmul(pows[m - 1 - j][0], pows[m - 1 - j][1], bbr, bbi)
        wr = spread(wr.reshape(S5_GROUPS * n, p), tile_p_ref, n, p)
        wi = spread(wi.reshape(S5_GROUPS * n, p), tile_p_ref, n, p)
        vr, vi = _cmul(pows_t[j + 1][0], pows_t[j + 1][1], crt, cit)
        vr = spread(vr.reshape(S5_GROUPS * p, n), tile_n_ref, p, n)
        vi = spread(-vi.reshape(S5_GROUPS * p, n), tile_n_ref, p, n)
        car, cai = _cmul(pows[j][0], pows[j][1], cr, ci)
        feed = (lax.dot_general(bbr, car, contract_states, preferred_element_type=F32,
                                precision=lax.Precision.HIGHEST)
                - lax.dot_general(bbi, cai, contract_states, preferred_element_type=F32,
                                  precision=lax.Precision.HIGHEST))
        feed = spread(feed.reshape(S5_GROUPS * n, n), tile_n_ref, n, n)
        for cb in range(S5_BLOCKS):
            rb = slice(cb * LANES, (cb + 1) * LANES)
            rc = slice(cb * S5_BLOCK_STATES, (cb + 1) * S5_BLOCK_STATES)
            wb_ref[cb, j * LANES:(j + 1) * LANES, 0:S5_BLOCK_STATES] = wr[rb]
            wb_ref[cb, j * LANES:(j + 1) * LANES, S5_BLOCK_STATES:2 * S5_BLOCK_STATES] = wi[rb]
            wc_ref[cb, 0:S5_BLOCK_STATES, j * LANES:(j + 1) * LANES] = vr[rc]
            wc_ref[cb, S5_BLOCK_STATES:2 * S5_BLOCK_STATES, j * LANES:(j + 1) * LANES] = vi[rc]
            for i in range(m - j):
                wf_ref[cb, i * LANES:(i + 1) * LANES, (i + j) * LANES:(i + j + 1) * LANES] = feed[rb]


def _s5_operands(lam_re, lam_im, log_dt, b_re, b_im, c_re, c_im):
    g, p, n, m, nb = S5_GROUPS, S5_STATE, S5_GROUP, S5_STEP_BLOCK, S5_BLOCKS
    ng = LANES // n
    vec = jax.ShapeDtypeStruct((g, 1, p), F32)
    tile_p = jnp.asarray(np.tile(np.eye(p, dtype=np.float32), (1, ng)), dtype=BF16)
    tile_n = jnp.asarray(np.tile(np.eye(n, dtype=np.float32), (1, ng)), dtype=BF16)
    return pl.pallas_call(
        _s5_prep_kernel,
        out_shape=(vec, vec,
                   jax.ShapeDtypeStruct((nb, m * LANES, 2 * S5_BLOCK_STATES), BF16),
                   jax.ShapeDtypeStruct((nb, 2 * S5_BLOCK_STATES, m * LANES), BF16),
                   jax.ShapeDtypeStruct((nb, m * LANES, m * LANES), BF16)),
        compiler_params=pltpu.CompilerParams(vmem_limit_bytes=VMEM_LIMIT_BYTES),
        name="s5_operands",
    )(lam_re.reshape(g, 1, p), lam_im.reshape(g, 1, p), log_dt.reshape(g, 1, 1),
      lam_re.reshape(g, p, 1), lam_im.reshape(g, p, 1),
      b_re.transpose(0, 2, 1), b_im.transpose(0, 2, 1), c_re, c_im,
      c_re.transpose(0, 2, 1), c_im.transpose(0, 2, 1), tile_p, tile_n)


def _s5_kernel(x_hbm, mod_ref, win_ref, wb_ref, amr_ref, ami_ref, wc_ref, wf_ref, d_ref, wglu_ref,
               o_hbm, xbuf, obuf, sem_in, sem_out, state_ref, h_ref, u_ref, ub_ref, bu_ref, s_ref,
               y_ref):
    m = S5_STEP_BLOCK
    nblk = TM_STEPS // m
    rows_blk = nblk * BATCH
    ns = S5_BLOCK_STATES
    slot_io = _fetch_tile(x_hbm, xbuf, sem_in)

    @pl.when(pl.program_id(0) == 0)
    def _():
        state_ref[...] = jnp.zeros_like(state_ref)

    x = xbuf[slot_io]
    shift = mod_ref[:, 0:D_MODEL][None]
    scale = mod_ref[:, D_MODEL:2 * D_MODEL][None]
    res_gate = mod_ref[:, 2 * D_MODEL:3 * D_MODEL][None]
    h_ref[...] = _norm_mod(x, scale, shift).reshape(TM_ROWS, D_MODEL).astype(BF16)
    u_ref[...] = jnp.dot(h_ref[...], win_ref[...], preferred_element_type=F32)
    _prefetch_next_tile(x_hbm, xbuf, sem_in, slot_io)
    ub_ref[...] = u_ref[...].astype(BF16).reshape(nblk, m, BATCH, D_MODEL)

    def block_inputs(cb):
        cols = slice(cb * LANES, (cb + 1) * LANES)
        return jnp.concatenate(
            [ub_ref[:, j, :, cols].reshape(rows_blk, LANES) for j in range(m)], axis=-1)

    def driving_term(cb):
        bu_ref[cb % S5_LOOKAHEAD] = jnp.dot(block_inputs(cb), wb_ref[cb],
                                            preferred_element_type=F32)

    def scan(cb):
        slot = cb % 2
        bu_slot = cb % S5_LOOKAHEAD
        ar = amr_ref[cb]
        ai = ami_ref[cb]
        sr = state_ref[cb, :, 0:ns]
        si = state_ref[cb, :, ns:2 * ns]
        for k in range(nblk):
            rows = slice(k * BATCH, (k + 1) * BATCH)
            s_ref[slot, rows, 0:ns] = sr.astype(BF16)
            s_ref[slot, rows, ns:2 * ns] = si.astype(BF16)
            sr, si = (ar * sr - ai * si + bu_ref[bu_slot, rows, 0:ns],
                      ar * si + ai * sr + bu_ref[bu_slot, rows, ns:2 * ns])
        state_ref[cb, :, 0:ns] = sr
        state_ref[cb, :, ns:2 * ns] = si

    def read_out(cb):
        cols = slice(cb * LANES, (cb + 1) * LANES)
        y_blk = (jnp.dot(s_ref[cb % 2], wc_ref[cb], preferred_element_type=F32)
                 + jnp.dot(block_inputs(cb), wf_ref[cb], preferred_element_type=F32))
        for j in range(m):
            y_ref[:, j, :, cols] = y_blk[:, j * LANES:(j + 1) * LANES].reshape(nblk, BATCH, LANES)

    for cb in range(S5_LOOKAHEAD):
        driving_term(cb)
    for cb in range(S5_BLOCKS + 1):
        if cb < S5_BLOCKS:
            scan(cb)
        if cb >= 1:
            read_out(cb - 1)
        if cb + S5_LOOKAHEAD < S5_BLOCKS:
            driving_term(cb + S5_LOOKAHEAD)

    y = y_ref[...].reshape(TM_ROWS, D_MODEL) + d_ref[...] * u_ref[...]
    h_ref[...] = jax.nn.gelu(y).astype(BF16)
    for c0 in range(0, D_MODEL, MXU_TILE):
        cols = slice(c0, c0 + MXU_TILE)
        gcols = slice(D_MODEL + c0, D_MODEL + c0 + MXU_TILE)
        val = jnp.dot(h_ref[...], wglu_ref[:, cols], preferred_element_type=F32)
        gate = jnp.dot(h_ref[...], wglu_ref[:, gcols], preferred_element_type=F32)
        mixed = (val * _sigmoid(gate)).reshape(TM_STEPS, BATCH, MXU_TILE)
        obuf[slot_io, :, :, cols] = xbuf[slot_io, :, :, cols] + res_gate[:, :, cols] * mixed
    _emit_tile(o_hbm, obuf, sem_out, slot_io, batch_major=False)
    _drain_prefetch(x_hbm, xbuf, sem_in, slot_io)


def _s5_mixer(xt, mod, w_in, lam_re, lam_im, log_dt, b_re, b_im, c_re, c_im, d_skip, w_glu):
    amr, ami, wb, wc, wf = _s5_operands(lam_re, lam_im, log_dt, b_re, b_im, c_re, c_im)
    nb = S5_BLOCKS
    m = S5_STEP_BLOCK
    nblk = TM_STEPS // m
    rows_blk = nblk * BATCH
    am_re = jnp.broadcast_to(amr.reshape(nb, 1, S5_BLOCK_STATES), (nb, BATCH, S5_BLOCK_STATES))
    am_im = jnp.broadcast_to(ami.reshape(nb, 1, S5_BLOCK_STATES), (nb, BATCH, S5_BLOCK_STATES))

    return pl.pallas_call(
        _s5_kernel,
        out_shape=jax.ShapeDtypeStruct((SEQ, BATCH * D_MODEL), F32),
        grid=(SEQ // TM_STEPS,),
        in_specs=[
            pl.BlockSpec(memory_space=pl.ANY),
            _resident((BATCH, 6 * D_MODEL)),
            _resident((D_MODEL, D_MODEL)),
            _resident((nb, m * LANES, 2 * S5_BLOCK_STATES)),
            _resident((nb, BATCH, S5_BLOCK_STATES)),
            _resident((nb, BATCH, S5_BLOCK_STATES)),
            _resident((nb, 2 * S5_BLOCK_STATES, m * LANES)),
            _resident((nb, m * LANES, m * LANES)),
            _resident((1, D_MODEL)),
            _resident((D_MODEL, 2 * D_MODEL)),
        ],
        out_specs=pl.BlockSpec(memory_space=pl.ANY),
        scratch_shapes=_TILE_SCRATCH + [
            pltpu.VMEM((nb, BATCH, 2 * S5_BLOCK_STATES), F32),
            pltpu.VMEM((TM_ROWS, D_MODEL), BF16),
            pltpu.VMEM((TM_ROWS, D_MODEL), F32),
            pltpu.VMEM((nblk, m, BATCH, D_MODEL), BF16),
            pltpu.VMEM((S5_LOOKAHEAD, rows_blk, 2 * S5_BLOCK_STATES), F32),
            pltpu.VMEM((2, rows_blk, 2 * S5_BLOCK_STATES), BF16),
            pltpu.VMEM((nblk, m, BATCH, D_MODEL), F32),
        ],
        compiler_params=_params(1),
        name="s5_mixer",
    )(xt, mod, w_in, wb, am_re, am_im, wc, wf, d_skip.reshape(1, D_MODEL), w_glu)


def kernel(x, c, pos, ada_w, ada_b, ret_w_in, ret_w_out, s5_w_in, s5_lam_re, s5_lam_im, s5_log_dt,
           s5_b_re, s5_b_im, s5_c_re, s5_c_im, s5_d, s5_w_glu, ffn_w_up, ffn_conv_w, ffn_conv_b,
           ffn_w_down, final_norm_g):
    mod = _modulation(c, ada_w, ada_b)
    half = RET_DK // 2
    inv_freq = jnp.power(ROPE_BASE, -jnp.arange(half, dtype=F32) / half).reshape(1, half)

    q, k, v, sg = _ret_projection(x, pos, mod[0], inv_freq, ret_w_in[0].astype(BF16))
    xt = _retention_core(q, k, v, sg, x, mod[0], ret_w_out[0].astype(BF16))
    ffn_params = (mod, ffn_w_up.astype(BF16), ffn_conv_w, ffn_conv_b, ffn_w_down.astype(BF16),
                  final_norm_g)
    xt = _conv_ffn(xt, 0, *ffn_params)
    xt = _s5_mixer(xt, mod[1], s5_w_in[0].astype(BF16), s5_lam_re[0], s5_lam_im[0], s5_log_dt[0],
                   s5_b_re[0], s5_b_im[0], s5_c_re[0], s5_c_im[0], s5_d[0],
                   s5_w_glu[0].astype(BF16))
    return _conv_ffn(xt, 1, *ffn_params)
```

```python
import functools

import jax
import jax.numpy as jnp
import numpy as np
from jax import lax
from jax.experimental import pallas as pl
from jax.experimental.pallas import tpu as pltpu

F32 = jnp.float32
BF16 = jnp.bfloat16

D_MODEL = 1024
BATCH = 16
SEQ = 4096
DEPTH = 2
RET_HEADS = 4
RET_DK = 256
RET_DV = 512
RET_QK = RET_HEADS * RET_DK
RET_V = RET_HEADS * RET_DV
RET_PROJ = 2 * RET_QK + 2 * RET_V
RET_CHUNK = 256
ROPE_BASE = 10000.0
S5_GROUP = 16
S5_GROUPS = 64
S5_STATE = 64
D_FF = 2816
CONV_W = 3
EPS = 1e-6

LANES = 128
SUBLANES = 8
VMEM_LIMIT_BYTES = 56 * 1024 * 1024

PROJ_ROWS = 512
RET_ROWS = 512
TM_STEPS = 32
TM_ROWS = TM_STEPS * BATCH
MXU_TILE = 256
FF_CHUNK = 3 * MXU_TILE
FF_CHUNKS = tuple((c, min(c + FF_CHUNK, D_FF)) for c in range(0, D_FF, FF_CHUNK))
CONV_HALO = (CONV_W - 1) * BATCH
S5_BLOCKS = D_MODEL // LANES
S5_BLOCK_STATES = (LANES // S5_GROUP) * S5_STATE
S5_LOOKAHEAD = 3
S5_STEP_BLOCK = 4
MOD_COLS = 3072


def _resident(shape):
    zeros = (0,) * len(shape)
    return pl.BlockSpec(shape, lambda *_: zeros, pipeline_mode=pl.Buffered(1))


def _resident_layer(shape, layer):
    zeros = (0,) * len(shape)
    return pl.BlockSpec((None,) + tuple(shape), lambda *_: (layer,) + zeros,
                        pipeline_mode=pl.Buffered(1))


def _params(n_axes):
    return pltpu.CompilerParams(
        dimension_semantics=("arbitrary",) * n_axes,
        vmem_limit_bytes=VMEM_LIMIT_BYTES,
    )


def _sigmoid(x):
    return jax.nn.sigmoid(x)


def _tile_in_copy(x_hbm, xbuf, sem, step, slot, b):
    src = x_hbm.at[pl.ds(step * TM_STEPS, TM_STEPS), pl.ds(b * D_MODEL, D_MODEL)]
    return pltpu.make_async_copy(src, xbuf.at[slot, :, b, :], sem.at[slot])


def _tile_out_copy(o_hbm, obuf, sem, step, slot, b, batch_major):
    if batch_major:
        dst = o_hbm.at[b, pl.ds(step * TM_STEPS, TM_STEPS), :]
    else:
        dst = o_hbm.at[pl.ds(step * TM_STEPS, TM_STEPS), pl.ds(b * D_MODEL, D_MODEL)]
    return pltpu.make_async_copy(obuf.at[slot, :, b, :], dst, sem.at[slot])


def _fetch_tile(x_hbm, xbuf, sem):
    i = pl.program_id(0)
    slot = lax.rem(i, 2)

    @pl.when(i == 0)
    def _():
        for b in range(BATCH):
            _tile_in_copy(x_hbm, xbuf, sem, 0, 0, b).start()

    for b in range(BATCH):
        _tile_in_copy(x_hbm, xbuf, sem, i, slot, b).wait()
    return slot


def _prefetch_next_tile(x_hbm, xbuf, sem, slot):
    i = pl.program_id(0)
    nxt = jnp.minimum(i + 1, pl.num_programs(0) - 1)
    for b in range(BATCH):
        _tile_in_copy(x_hbm, xbuf, sem, nxt, 1 - slot, b).start()


def _drain_prefetch(x_hbm, xbuf, sem, slot):
    i = pl.program_id(0)

    @pl.when(i == pl.num_programs(0) - 1)
    def _():
        for b in range(BATCH):
            _tile_in_copy(x_hbm, xbuf, sem, i, 1 - slot, b).wait()


def _emit_tile(o_hbm, obuf, sem, slot, batch_major):
    i = pl.program_id(0)
    for b in range(BATCH):
        _tile_out_copy(o_hbm, obuf, sem, i, slot, b, batch_major).start()

    @pl.when(i >= 1)
    def _():
        for b in range(BATCH):
            _tile_out_copy(o_hbm, obuf, sem, i - 1, 1 - slot, b, batch_major).wait()

    @pl.when(i == pl.num_programs(0) - 1)
    def _():
        for b in range(BATCH):
            _tile_out_copy(o_hbm, obuf, sem, i, slot, b, batch_major).wait()


_TILE_SCRATCH = [
    pltpu.VMEM((2, TM_STEPS, BATCH, D_MODEL), F32),
    pltpu.VMEM((2, TM_STEPS, BATCH, D_MODEL), F32),
    pltpu.SemaphoreType.DMA((2,)),
    pltpu.SemaphoreType.DMA((2,)),
]


def _norm_mod(x, scale, shift):
    ms = jnp.mean(x * x, axis=-1, keepdims=True)
    return x * lax.rsqrt(ms + EPS) * (1.0 + scale) + shift


def _mod_kernel(c_ref, w_ref, b_ref, o_ref):
    c = c_ref[...]
    cond = c * _sigmoid(c)
    o_ref[0] = jnp.dot(cond, w_ref[0], preferred_element_type=F32,
                       precision=lax.Precision.HIGHEST) + b_ref[0]


def _modulation(c, ada_w, ada_b):
    n_cols = 6 * D_MODEL
    return pl.pallas_call(
        _mod_kernel,
        out_shape=jax.ShapeDtypeStruct((DEPTH, BATCH, n_cols), F32),
        grid=(DEPTH, n_cols // MOD_COLS),
        in_specs=[
            pl.BlockSpec((BATCH, D_MODEL), lambda i, j: (0, 0)),
            pl.BlockSpec((1, D_MODEL, MOD_COLS), lambda i, j: (i, 0, j)),
            pl.BlockSpec((1, 1, MOD_COLS), lambda i, j: (i, 0, j)),
        ],
        out_specs=pl.BlockSpec((1, BATCH, MOD_COLS), lambda i, j: (i, 0, j)),
        compiler_params=_params(2),
        name="adaln_modulation",
    )(c, ada_w, ada_b.reshape(DEPTH, 1, n_cols))


def _rotary_table_part(pos_ref, invf_ref, rot_ref, rows):
    ang = pos_ref[0, rows, :].astype(F32) * invf_ref[...]
    cos = jnp.cos(ang)
    sin = jnp.sin(ang)
    k_scale = RET_DK ** -0.5
    rot_ref[0, rows, :] = cos
    rot_ref[1, rows, :] = sin
    rot_ref[2, rows, :] = cos * k_scale
    rot_ref[3, rows, :] = sin * k_scale
    bits = pltpu.bitcast(cos, jnp.uint32) | pltpu.bitcast(sin, jnp.uint32)
    folded = bits[0:2 * SUBLANES]
    for r0 in range(2 * SUBLANES, bits.shape[0], 2 * SUBLANES):
        folded = folded | bits[r0:r0 + 2 * SUBLANES]
    zeros = lax.shift_right_logical(lax.shift_right_logical(folded, jnp.uint32(16)), jnp.uint32(16))
    return zeros.astype(F32)


def _ret_proj_kernel(x_ref, pos_ref, mod_ref, invf_ref, w_ref,
                     q_ref, k_ref, v_ref, sg_ref, h_ref, rot_ref):
    shift = mod_ref[0, :, 0:D_MODEL]
    scale = mod_ref[0, :, D_MODEL:2 * D_MODEL]
    h_ref[...] = _norm_mod(x_ref[0], scale, shift).astype(BF16)

    part = PROJ_ROWS // (2 * RET_HEADS)
    head = slice(0, 2 * SUBLANES)

    def table_part(j):
        return _rotary_table_part(pos_ref, invf_ref, rot_ref, slice(j * part, (j + 1) * part))

    for hd in range(RET_HEADS):
        c0 = 2 * RET_QK + hd * RET_DV
        cols = slice(hd * RET_DV, (hd + 1) * RET_DV)
        y = jnp.dot(h_ref[...], w_ref[:, c0:c0 + RET_DV], preferred_element_type=F32)
        v_ref[0, :, cols] = y.astype(BF16)
        v_ref[0, head, hd * RET_DV:hd * RET_DV + LANES] = (y[head, 0:LANES] + table_part(hd)).astype(BF16)
    for hd in range(RET_HEADS):
        c0 = 2 * RET_QK + RET_V + hd * RET_DV
        cols = slice(hd * RET_DV, (hd + 1) * RET_DV)
        g = jnp.dot(h_ref[...], w_ref[:, c0:c0 + RET_DV], preferred_element_type=F32)
        sg = g * _sigmoid(g)
        sg_ref[0, :, cols] = sg.astype(BF16)
        sg_ref[0, head, hd * RET_DV:hd * RET_DV + LANES] = (
            sg[head, 0:LANES] + table_part(RET_HEADS + hd)).astype(BF16)

    half = RET_DK // 2
    for hd in range(RET_HEADS):
        for dst, base, table in ((q_ref, 0, 0), (k_ref, RET_QK, 2)):
            c0 = base + hd * RET_DK
            y = jnp.dot(h_ref[...], w_ref[:, c0:c0 + RET_DK], preferred_element_type=F32)
            t1 = y[:, :half]
            t2 = y[:, half:]
            cs = rot_ref[table]
            sn = rot_ref[table + 1]
            o0 = hd * RET_DK
            dst[0, :, o0:o0 + half] = (t1 * cs - t2 * sn).astype(BF16)
            dst[0, :, o0 + half:o0 + RET_DK] = (t1 * sn + t2 * cs).astype(BF16)


def _ret_projection(x, pos, mod0, inv_freq, w_in):
    rows = PROJ_ROWS
    steps = SEQ // rows
    row_map = lambda b, t: (b, t, 0)
    return pl.pallas_call(
        _ret_proj_kernel,
        out_shape=(
            jax.ShapeDtypeStruct((BATCH, SEQ, RET_QK), BF16),
            jax.ShapeDtypeStruct((BATCH, SEQ, RET_QK), BF16),
            jax.ShapeDtypeStruct((BATCH, SEQ, RET_V), BF16),
            jax.ShapeDtypeStruct((BATCH, SEQ, RET_V), BF16),
        ),
        grid=(BATCH, steps),
        in_specs=[
            pl.BlockSpec((1, rows, D_MODEL), row_map),
            pl.BlockSpec((1, rows, 1), row_map),
            pl.BlockSpec((1, 1, 6 * D_MODEL), lambda b, t: (b, 0, 0)),
            _resident((1, RET_DK // 2)),
            _resident((D_MODEL, RET_PROJ)),
        ],
        out_specs=(
            pl.BlockSpec((1, rows, RET_QK), row_map),
            pl.BlockSpec((1, rows, RET_QK), row_map),
            pl.BlockSpec((1, rows, RET_V), row_map),
            pl.BlockSpec((1, rows, RET_V), row_map),
        ),
        scratch_shapes=[
            pltpu.VMEM((rows, D_MODEL), BF16),
            pltpu.VMEM((4, rows, RET_DK // 2), F32),
        ],
        compiler_params=_params(2),
        name="retention_projection",
    )(x, pos.reshape(BATCH, SEQ, 1), mod0.reshape(BATCH, 1, 6 * D_MODEL), inv_freq, w_in)


def _ret_core_kernel(q_ref, k_ref, v_ref, sg_ref, x_ref, mod_ref, intra_ref, cross_ref,
                     tost_ref, decay_ref, wout_ref, o_ref, state_ref, go_ref):
    @pl.when(pl.program_id(1) == 0)
    def _():
        state_ref[...] = jnp.zeros_like(state_ref)

    contract_last = (((1,), (1,)), ((), ()))
    contract_first = (((0,), (0,)), ((), ()))
    for c in range(RET_ROWS // RET_CHUNK):
        rows = slice(c * RET_CHUNK, (c + 1) * RET_CHUNK)
        for hd in range(RET_HEADS):
            qk_cols = slice(hd * RET_DK, (hd + 1) * RET_DK)
            v_cols = slice(hd * RET_DV, (hd + 1) * RET_DV)
            qc = q_ref[0, rows, qk_cols]
            kc = k_ref[0, rows, qk_cols]
            vc = v_ref[0, rows, v_cols]
            state = state_ref[hd]
            s = lax.dot_general(qc, kc, contract_last, preferred_element_type=F32) * intra_ref[hd]
            o = jnp.dot(s.astype(BF16), vc, preferred_element_type=F32)
            o = o + jnp.dot(qc, state.astype(BF16), preferred_element_type=F32) * cross_ref[hd]
            kd = (kc.astype(F32) * tost_ref[hd]).astype(BF16)
            state_ref[hd] = decay_ref[hd] * state + lax.dot_general(
                kd, vc, contract_first, preferred_element_type=F32)
            mu = jnp.mean(o, axis=-1, keepdims=True)
            oc = o - mu
            on = oc * lax.rsqrt(jnp.mean(oc * oc, axis=-1, keepdims=True) + EPS)
            go_ref[rows, v_cols] = (sg_ref[0, rows, v_cols].astype(F32) * on).astype(BF16)

    y = jnp.dot(go_ref[...], wout_ref[...], preferred_element_type=F32)
    gate = mod_ref[0, :, 2 * D_MODEL:3 * D_MODEL]
    o_ref[...] = x_ref[0] + gate * y


def _retention_core(q, k, v, sg, x, mod0, w_out):
    log_gamma = np.log1p(-np.exp2(-5.0 - np.arange(RET_HEADS, dtype=np.float64)))
    idx = np.arange(RET_CHUNK, dtype=np.float64)
    diff = idx[:, None] - idx[None, :]
    intra = np.where(diff[None] >= 0,
                     np.exp(log_gamma[:, None, None] * np.maximum(diff, 0.0)[None]), 0.0)
    cross = np.exp(log_gamma[:, None] * (idx + 1.0))[:, :, None]
    to_state = np.exp(log_gamma[:, None] * (RET_CHUNK - 1.0 - idx))[:, :, None]
    chunk_decay = np.exp(log_gamma * RET_CHUNK)
    intra, cross, to_state, chunk_decay = (
        jnp.asarray(t, dtype=F32) for t in (intra, cross, to_state, chunk_decay))

    rows = RET_ROWS
    row_map = lambda b, t: (b, t, 0)
    return pl.pallas_call(
        _ret_core_kernel,
        out_shape=jax.ShapeDtypeStruct((SEQ, BATCH * D_MODEL), F32),
        grid=(BATCH, SEQ // rows),
        in_specs=[
            pl.BlockSpec((1, rows, RET_QK), row_map),
            pl.BlockSpec((1, rows, RET_QK), row_map),
            pl.BlockSpec((1, rows, RET_V), row_map),
            pl.BlockSpec((1, rows, RET_V), row_map),
            pl.BlockSpec((1, rows, D_MODEL), row_map),
            pl.BlockSpec((1, 1, 6 * D_MODEL), lambda b, t: (b, 0, 0)),
            _resident((RET_HEADS, RET_CHUNK, RET_CHUNK)),
            _resident((RET_HEADS, RET_CHUNK, 1)),
            _resident((RET_HEADS, RET_CHUNK, 1)),
            pl.BlockSpec(memory_space=pltpu.SMEM),
            _resident((RET_V, D_MODEL)),
        ],
        out_specs=pl.BlockSpec((rows, D_MODEL), lambda b, t: (t, b)),
        scratch_shapes=[
            pltpu.VMEM((RET_HEADS, RET_DK, RET_DV), F32),
            pltpu.VMEM((rows, RET_V), BF16),
        ],
        compiler_params=_params(2),
        name="retention_core",
    )(q, k, v, sg, x, mod0.reshape(BATCH, 1, 6 * D_MODEL), intra, cross, to_state,
      chunk_decay, w_out)


def _ffn_kernel(x_hbm, mod_ref, wup_ref, cw_ref, cb_ref, wdn_ref, fg_ref, o_hbm,
                xbuf, obuf, sem_in, sem_out, h_ref, gate_ref, *, last):
    slot = _fetch_tile(x_hbm, xbuf, sem_in)

    @pl.when(pl.program_id(0) == 0)
    def _():
        gate_ref[0:CONV_HALO, :] = jnp.zeros((CONV_HALO, D_FF), F32)

    x = xbuf[slot]
    shift = mod_ref[:, 3 * D_MODEL:4 * D_MODEL][None]
    scale = mod_ref[:, 4 * D_MODEL:5 * D_MODEL][None]
    res_gate = mod_ref[:, 5 * D_MODEL:6 * D_MODEL][None]
    h_ref[...] = _norm_mod(x, scale, shift).reshape(TM_ROWS, D_MODEL).astype(BF16)

    def up_projection(c0, c1):
        gate_ref[CONV_HALO:CONV_HALO + TM_ROWS, c0:c1] = jnp.dot(
            h_ref[...], wup_ref[:, D_FF + c0:D_FF + c1], preferred_element_type=F32)
        return jnp.dot(h_ref[...], wup_ref[:, c0:c1], preferred_element_type=F32)

    acc = None
    val_next = up_projection(*FF_CHUNKS[0])
    _prefetch_next_tile(x_hbm, xbuf, sem_in, slot)
    for idx, (c0, c1) in enumerate(FF_CHUNKS):
        val = val_next
        if idx + 1 < len(FF_CHUNKS):
            val_next = up_projection(*FF_CHUNKS[idx + 1])
        conv = cb_ref[:, c0:c1]
        for tap in range(CONV_W):
            conv = conv + cw_ref[tap:tap + 1, c0:c1] * gate_ref[tap * BATCH:tap * BATCH + TM_ROWS, c0:c1]
        act = (conv * _sigmoid(conv) * val).astype(BF16)
        part = jnp.dot(act, wdn_ref[c0:c1, :], preferred_element_type=F32)
        acc = part if acc is None else acc + part

    gate_ref[0:CONV_HALO, :] = gate_ref[TM_ROWS:TM_ROWS + CONV_HALO, :]
    out = x + res_gate * acc.reshape(TM_STEPS, BATCH, D_MODEL)
    if last:
        ms = jnp.mean(out * out, axis=-1, keepdims=True)
        out = out * lax.rsqrt(ms + EPS) * fg_ref[...][None]
    obuf[slot] = out
    _emit_tile(o_hbm, obuf, sem_out, slot, batch_major=last)
    _drain_prefetch(x_hbm, xbuf, sem_in, slot)


def _conv_ffn(xt, layer, mod, w_up, conv_w, conv_b, w_down, final_g):
    last = layer == DEPTH - 1
    if last:
        out_shape = jax.ShapeDtypeStruct((BATCH, SEQ, D_MODEL), F32)
    else:
        out_shape = jax.ShapeDtypeStruct((SEQ, BATCH * D_MODEL), F32)
    return pl.pallas_call(
        functools.partial(_ffn_kernel, last=last),
        out_shape=out_shape,
        grid=(SEQ // TM_STEPS,),
        in_specs=[
            pl.BlockSpec(memory_space=pl.ANY),
            _resident_layer((BATCH, 6 * D_MODEL), layer),
            _resident_layer((D_MODEL, 2 * D_FF), layer),
            _resident_layer((CONV_W, D_FF), layer),
            _resident_layer((1, D_FF), layer),
            _resident_layer((D_FF, D_MODEL), layer),
            _resident((1, D_MODEL)),
        ],
        out_specs=pl.BlockSpec(memory_space=pl.ANY),
        scratch_shapes=_TILE_SCRATCH + [
            pltpu.VMEM((TM_ROWS, D_MODEL), BF16),
            pltpu.VMEM((CONV_HALO + TM_ROWS, D_FF), F32),
        ],
        compiler_params=_params(1),
        name="conv_ffn_last" if last else "conv_ffn",
    )(xt, mod, w_up, conv_w.reshape(DEPTH, CONV_W, D_FF), conv_b.reshape(DEPTH, 1, D_FF), w_down,
      final_g.reshape(1, D_MODEL))


def _cmul(xr, xi, yr, yi):
    return xr * yr - xi * yi, xr * yi + xi * yr


def _s5_zoh(lr, li, dt):
    mag = jnp.exp(lr * dt)
    ar = mag * jnp.cos(li * dt)
    ai = mag * jnp.sin(li * dt)
    nr = ar - 1.0
    den = lr * lr + li * li
    return ar, ai, (nr * lr + ai * li) / den, (ai * lr - nr * li) / den


def _powers(ar, ai, n):
    pows = [(jnp.ones_like(ar), jnp.zeros_like(ar))]
    for _ in range(n):
        pows.append(_cmul(pows[-1][0], pows[-1][1], ar, ai))
    return pows


def _s5_prep_kernel(lr_ref, li_ref, ldt_ref, lrt_ref, lit_ref, br_ref, bi_ref, cr_ref, ci_ref,
                    crt_ref, cit_ref, tile_p_ref, tile_n_ref,
                    amr_ref, ami_ref, wb_ref, wc_ref, wf_ref):
    m = S5_STEP_BLOCK
    ng = LANES // S5_GROUP
    dt = jnp.exp(ldt_ref[...])
    ar, ai, fr, fi = _s5_zoh(lr_ref[...], li_ref[...], dt)
    bbr, bbi = _cmul(fr, fi, br_ref[...], bi_ref[...])
    pows = _powers(ar, ai, m)
    amr_ref[...], ami_ref[...] = pows[m]
    art, ait, _, _ = _s5_zoh(lrt_ref[...], lit_ref[...], dt)
    pows_t = _powers(art, ait, m)

    def spread(block, tile_ref, rows_per_group, cols_per_group):
        wide = jnp.dot(block.astype(BF16), tile_ref[...], preferred_element_type=F32)
        rows = lax.broadcasted_iota(jnp.int32, wide.shape, 0)
        cols = lax.broadcasted_iota(jnp.int32, wide.shape, 1)
        row_group = lax.shift_right_logical(rows, rows_per_group.bit_length() - 1) & (ng - 1)
        col_group = lax.shift_right_logical(cols, cols_per_group.bit_length() - 1)
        return jnp.where(row_group == col_group, wide, 0.0).astype(BF16)

    n, p = S5_GROUP, S5_STATE
    cr, ci = cr_ref[...], ci_ref[...]
    crt, cit = crt_ref[...], cit_ref[...]
    contract_states = (((2,), (2,)), ((0,), (0,)))
    wf_ref[...] = jnp.zeros_like(wf_ref)
    for j in range(m):
        wr, wi = _cmul(pows[m - 1 - j][0], pows[m - 1 - j][1], bbr, bbi)
        wr = spread(wr.reshape(S5_GROUPS * n, p), tile_p_ref, n, p)
        wi = spread(wi.reshape(S5_GROUPS * n, p), tile_p_ref, n, p)
        vr, vi = _cmul(pows_t[j + 1][0], pows_t[j + 1][1], crt, cit)
        vr = spread(vr.reshape(S5_GROUPS * p, n), tile_n_ref, p, n)
        vi = spread(-vi.reshape(S5_GROUPS * p, n), tile_n_ref, p, n)
        car, cai = _cmul(pows[j][0], pows[j][1], cr, ci)
        feed = (lax.dot_general(bbr, car, contract_states, preferred_element_type=F32,
                                precision=lax.Precision.HIGHEST)
                - lax.dot_general(bbi, cai, contract_states, preferred_element_type=F32,
                                  precision=lax.Precision.HIGHEST))
        feed = spread(feed.reshape(S5_GROUPS * n, n), tile_n_ref, n, n)
        for cb in range(S5_BLOCKS):
            rb = slice(cb * LANES, (cb + 1) * LANES)
            rc = slice(cb * S5_BLOCK_STATES, (cb + 1) * S5_BLOCK_STATES)
            wb_ref[cb, j * LANES:(j + 1) * LANES, 0:S5_BLOCK_STATES] = wr[rb]
            wb_ref[cb, j * LANES:(j + 1) * LANES, S5_BLOCK_STATES:2 * S5_BLOCK_STATES] = wi[rb]
            wc_ref[cb, 0:S5_BLOCK_STATES, j * LANES:(j + 1) * LANES] = vr[rc]
            wc_ref[cb, S5_BLOCK_STATES:2 * S5_BLOCK_STATES, j * LANES:(j + 1) * LANES] = vi[rc]
            for i in range(m - j):
                wf_ref[cb, i * LANES:(i + 1) * LANES, (i + j) * LANES:(i + j + 1) * LANES] = feed[rb]


def _s5_operands(lam_re, lam_im, log_dt, b_re, b_im, c_re, c_im):
    g, p, n, m, nb = S5_GROUPS, S5_STATE, S5_GROUP, S5_STEP_BLOCK, S5_BLOCKS
    ng = LANES // n
    vec = jax.ShapeDtypeStruct((g, 1, p), F32)
    tile_p = jnp.asarray(np.tile(np.eye(p, dtype=np.float32), (1, ng)), dtype=BF16)
    tile_n = jnp.asarray(np.tile(np.eye(n, dtype=np.float32), (1, ng)), dtype=BF16)
    return pl.pallas_call(
        _s5_prep_kernel,
        out_shape=(vec, vec,
                   jax.ShapeDtypeStruct((nb, m * LANES, 2 * S5_BLOCK_STATES), BF16),
                   jax.ShapeDtypeStruct((nb, 2 * S5_BLOCK_STATES, m * LANES), BF16),
                   jax.ShapeDtypeStruct((nb, m * LANES, m * LANES), BF16)),
        compiler_params=pltpu.CompilerParams(vmem_limit_bytes=VMEM_LIMIT_BYTES),
        name="s5_operands",
    )(lam_re.reshape(g, 1, p), lam_im.reshape(g, 1, p), log_dt.reshape(g, 1, 1),
      lam_re.reshape(g, p, 1), lam_im.reshape(g, p, 1),
      b_re.transpose(0, 2, 1), b_im.transpose(0, 2, 1), c_re, c_im,
      c_re.transpose(0, 2, 1), c_im.transpose(0, 2, 1), tile_p, tile_n)


def _s5_kernel(x_hbm, mod_ref, win_ref, wb_ref, amr_ref, ami_ref, wc_ref, wf_ref, d_ref, wglu_ref,
               o_hbm, xbuf, obuf, sem_in, sem_out, state_ref, h_ref, u_ref, ub_ref, bu_ref, s_ref,
               y_ref):
    m = S5_STEP_BLOCK
    nblk = TM_STEPS // m
    rows_blk = nblk * BATCH
    ns = S5_BLOCK_STATES
    slot_io = _fetch_tile(x_hbm, xbuf, sem_in)

    @pl.when(pl.program_id(0) == 0)
    def _():
        state_ref[...] = jnp.zeros_like(state_ref)

    x = xbuf[slot_io]
    shift = mod_ref[:, 0:D_MODEL][None]
    scale = mod_ref[:, D_MODEL:2 * D_MODEL][None]
    res_gate = mod_ref[:, 2 * D_MODEL:3 * D_MODEL][None]
    h_ref[...] = _norm_mod(x, scale, shift).reshape(TM_ROWS, D_MODEL).astype(BF16)
    u_ref[...] = jnp.dot(h_ref[...], win_ref[...], preferred_element_type=F32)
    _prefetch_next_tile(x_hbm, xbuf, sem_in, slot_io)
    ub_ref[...] = u_ref[...].astype(BF16).reshape(nblk, m, BATCH, D_MODEL)

    def block_inputs(cb):
        cols = slice(cb * LANES, (cb + 1) * LANES)
        return jnp.concatenate(
            [ub_ref[:, j, :, cols].reshape(rows_blk, LANES) for j in range(m)], axis=-1)

    def driving_term(cb):
        bu_ref[cb % S5_LOOKAHEAD] = jnp.dot(block_inputs(cb), wb_ref[cb],
                                            preferred_element_type=F32)

    def scan(cb):
        slot = cb % 2
        bu_slot = cb % S5_LOOKAHEAD
        ar = amr_ref[cb]
        ai = ami_ref[cb]
        sr = state_ref[cb, :, 0:ns]
        si = state_ref[cb, :, ns:2 * ns]
        for k in range(nblk):
            rows = slice(k * BATCH, (k + 1) * BATCH)
            s_ref[slot, rows, 0:ns] = sr.astype(BF16)
            s_ref[slot, rows, ns:2 * ns] = si.astype(BF16)
            sr, si = (ar * sr - ai * si + bu_ref[bu_slot, rows, 0:ns],
                      ar * si + ai * sr + bu_ref[bu_slot, rows, ns:2 * ns])
        state_ref[cb, :, 0:ns] = sr
        state_ref[cb, :, ns:2 * ns] = si

    def read_out(cb):
        cols = slice(cb * LANES, (cb + 1) * LANES)
        y_blk = (jnp.dot(s_ref[cb % 2], wc_ref[cb], preferred_element_type=F32)
                 + jnp.dot(block_inputs(cb), wf_ref[cb], preferred_element_type=F32))
        for j in range(m):
            y_ref[:, j, :, cols] = y_blk[:, j * LANES:(j + 1) * LANES].reshape(nblk, BATCH, LANES)

    for cb in range(S5_LOOKAHEAD):
        driving_term(cb)
    for cb in range(S5_BLOCKS + 1):
        if cb < S5_BLOCKS:
            scan(cb)
        if cb >= 1:
            read_out(cb - 1)
        if cb + S5_LOOKAHEAD < S5_BLOCKS:
            driving_term(cb + S5_LOOKAHEAD)

    y = y_ref[...].reshape(TM_ROWS, D_MODEL) + d_ref[...] * u_ref[...]
    h_ref[...] = jax.nn.gelu(y).astype(BF16)
    for c0 in range(0, D_MODEL, MXU_TILE):
        cols = slice(c0, c0 + MXU_TILE)
        gcols = slice(D_MODEL + c0, D_MODEL + c0 + MXU_TILE)
        val = jnp.dot(h_ref[...], wglu_ref[:, cols], preferred_element_type=F32)
        gate = jnp.dot(h_ref[...], wglu_ref[:, gcols], preferred_element_type=F32)
        mixed = (val * _sigmoid(gate)).reshape(TM_STEPS, BATCH, MXU_TILE)
        obuf[slot_io, :, :, cols] = xbuf[slot_io, :, :, cols] + res_gate[:, :, cols] * mixed
    _emit_tile(o_hbm, obuf, sem_out, slot_io, batch_major=False)
    _drain_prefetch(x_hbm, xbuf, sem_in, slot_io)


def _s5_mixer(xt, mod, w_in, lam_re, lam_im, log_dt, b_re, b_im, c_re, c_im, d_skip, w_glu):
    amr, ami, wb, wc, wf = _s5_operands(lam_re, lam_im, log_dt, b_re, b_im, c_re, c_im)
    nb = S5_BLOCKS
    m = S5_STEP_BLOCK
    nblk = TM_STEPS // m
    rows_blk = nblk * BATCH
    am_re = jnp.broadcast_to(amr.reshape(nb, 1, S5_BLOCK_STATES), (nb, BATCH, S5_BLOCK_STATES))
    am_im = jnp.broadcast_to(ami.reshape(nb, 1, S5_BLOCK_STATES), (nb, BATCH, S5_BLOCK_STATES))

    return pl.pallas_call(
        _s5_kernel,
        out_shape=jax.ShapeDtypeStruct((SEQ, BATCH * D_MODEL), F32),
        grid=(SEQ // TM_STEPS,),
        in_specs=[
            pl.BlockSpec(memory_space=pl.ANY),
            _resident((BATCH, 6 * D_MODEL)),
            _resident((D_MODEL, D_MODEL)),
            _resident((nb, m * LANES, 2 * S5_BLOCK_STATES)),
            _resident((nb, BATCH, S5_BLOCK_STATES)),
            _resident((nb, BATCH, S5_BLOCK_STATES)),
            _resident((nb, 2 * S5_BLOCK_STATES, m * LANES)),
            _resident((nb, m * LANES, m * LANES)),
            _resident((1, D_MODEL)),
            _resident((D_MODEL, 2 * D_MODEL)),
        ],
        out_specs=pl.BlockSpec(memory_space=pl.ANY),
        scratch_shapes=_TILE_SCRATCH + [
            pltpu.VMEM((nb, BATCH, 2 * S5_BLOCK_STATES), F32),
            pltpu.VMEM((TM_ROWS, D_MODEL), BF16),
            pltpu.VMEM((TM_ROWS, D_MODEL), F32),
            pltpu.VMEM((nblk, m, BATCH, D_MODEL), BF16),
            pltpu.VMEM((S5_LOOKAHEAD, rows_blk, 2 * S5_BLOCK_STATES), F32),
            pltpu.VMEM((2, rows_blk, 2 * S5_BLOCK_STATES), BF16),
            pltpu.VMEM((nblk, m, BATCH, D_MODEL), F32),
        ],
        compiler_params=_params(1),
        name="s5_mixer",
    )(xt, mod, w_in, wb, am_re, am_im, wc, wf, d_skip.reshape(1, D_MODEL), w_glu)


def kernel(x, c, pos, ada_w, ada_b, ret_w_in, ret_w_out, s5_w_in, s5_lam_re, s5_lam_im, s5_log_dt,
           s5_b_re, s5_b_im, s5_c_re, s5_c_im, s5_d, s5_w_glu, ffn_w_up, ffn_conv_w, ffn_conv_b,
           ffn_w_down, final_norm_g):
    mod = _modulation(c, ada_w, ada_b)
    half = RET_DK // 2
    inv_freq = jnp.power(ROPE_BASE, -jnp.arange(half, dtype=F32) / half).reshape(1, half)

    q, k, v, sg = _ret_projection(x, pos, mod[0], inv_freq, ret_w_in[0].astype(BF16))
    xt = _retention_core(q, k, v, sg, x, mod[0], ret_w_out[0].astype(BF16))
    ffn_params = (mod, ffn_w_up.astype(BF16), ffn_conv_w, ffn_conv_b, ffn_w_down.astype(BF16),
                  final_norm_g)
    xt = _conv_ffn(xt, 0, *ffn_params)
    xt = _s5_mixer(xt, mod[1], s5_w_in[0].astype(BF16), s5_lam_re[0], s5_lam_im[0], s5_log_dt[0],
                   s5_b_re[0], s5_b_im[0], s5_c_re[0], s5_c_im[0], s5_d[0],
                   s5_w_glu[0].astype(BF16))
    return _conv_ffn(xt, 1, *ffn_params)
```
